```python
import math
import jax, jax.numpy as jnp
from jax import lax
import numpy as np

D_MODEL = 2048
BATCH = 4
SEQ = 4096
DEPTH = 2

GRID_W = 64
CTX_LEN = 256
EPS = 1e-6
ATT_HEADS = 8
ATT_DQK = 64
ATT_DV = 2 * ATT_DQK
ATT_W = ATT_HEADS * ATT_DV
QK_COLS = ATT_HEADS * 2 * ATT_DQK
Q_BLOCK = 128
ROPE_BASE = 10000.0
CONF_W = 512
CONF_K = 31
HY_W = 512
HY_ORDER = 2
HY_SHORT_K = 3
HY_EMB = 33
HY_BANDS = (HY_EMB - 1) // 2
HY_FFN = 64
HY_FILT = HY_ORDER * 2 * HY_W
N_BRANCH = 3
SPLIT_POINTS = (QK_COLS, 2 * QK_COLS, 2 * QK_COLS + ATT_W, 2 * QK_COLS + ATT_W + 2 * CONF_W,
                2 * QK_COLS + ATT_W + 2 * CONF_W + (HY_ORDER + 1) * HY_W)
IN_COLS = SPLIT_POINTS[-1] + N_BRANCH * D_MODEL
N_GROUPS = 8
E_PER_GROUP = 8
N_EXPERTS = N_GROUPS * E_PER_GROUP
TOP_K = 2
D_EXPERT = 512
MOE_BLOCK = 128

kernel_name = "hybrid_diffattn_conformer_hyena_hmoe_dit"

F32 = jnp.float32


def rms_norm(x, g, eps=EPS):
    xf = x.astype(F32)
    y = xf * lax.rsqrt(jnp.mean(xf * xf, axis=-1, keepdims=True) + eps)
    return (y * g.astype(F32)).astype(x.dtype)


def layer_norm(x, g, b, eps=1e-5):
    xf = x.astype(F32)
    mu = jnp.mean(xf, axis=-1, keepdims=True)
    xc = xf - mu
    y = xc * lax.rsqrt(jnp.mean(xc * xc, axis=-1, keepdims=True) + eps)
    return (y * g.astype(F32) + b.astype(F32)).astype(x.dtype)


def ada_norm(x, g, shift, scale):
    return rms_norm(x, g) * (1 + scale) + shift


def depthwise_conv(x, w, b):
    C = x.shape[-1]
    y = lax.conv_general_dilated(x, w[:, None, :].astype(x.dtype), window_strides=(1,), padding='SAME',
                                 dimension_numbers=('NWC', 'WIO', 'NWC'), feature_group_count=C)
    return y + b.astype(x.dtype)


def axial_rope_tables(rows, cols):
    half = ATT_DQK // 2
    inv = ROPE_BASE ** (-jnp.arange(0, half, 2, dtype=F32) / half)
    ar = rows.astype(F32)[:, None] * inv
    ac = cols.astype(F32)[:, None] * inv
    return jnp.cos(ar), jnp.sin(ar), jnp.cos(ac), jnp.sin(ac)


def _rot_half(x, cos, sin):
    x1, x2 = jnp.split(x, 2, axis=-1)
    cos = cos[:, None, None, :]
    sin = sin[:, None, None, :]
    return jnp.concatenate([x1 * cos - x2 * sin, x1 * sin + x2 * cos], axis=-1).astype(x.dtype)


def apply_axial_rope(x, tables):
    cr, sr, cc, sc = tables
    xr, xc = jnp.split(x, 2, axis=-1)
    return jnp.concatenate([_rot_half(xr, cr, sr), _rot_half(xc, cc, sc)], axis=-1)


def diff_attention(q, k, v, lam):
    s = jnp.einsum('bqhcd,bkhcd->bhcqk', q, k).astype(F32) * (q.shape[-1] ** -0.5)
    p = jax.nn.softmax(s, axis=-1)
    a = p[:, :, 0] - lam * p[:, :, 1]
    return jnp.einsum('bhqk,bkhe->bqhe', a.astype(v.dtype), v)


def blocked_diff_attention(q, k, v, lam):
    B, S, H, _, dqk = q.shape
    nb = S // Q_BLOCK
    qb = jnp.moveaxis(q.reshape(B, nb, Q_BLOCK, H, 2, dqk), 1, 0)
    ob = lax.map(lambda qi: diff_attention(qi, k, v, lam), qb)
    return jnp.moveaxis(ob, 0, 1).reshape(B, S, H, v.shape[-1])


def diff_attn_out(o, subln_g, lam_init, w_o):
    B, L = o.shape[:2]
    o = rms_norm(o, subln_g, 1e-5) * (1.0 - lam_init)
    return o.reshape(B, L, ATT_W) @ w_o


def conformer_branch(u, dw_w, dw_b, ln_g, ln_b, w_o):
    a, g = jnp.split(u, 2, axis=-1)
    y = a * jax.nn.sigmoid(g)
    y = depthwise_conv(y, dw_w, dw_b)
    y = jax.nn.silu(layer_norm(y, ln_g, ln_b))
    return y @ w_o


def hyena_filter_fft(L, w1, b1, freq, w2, b2, w3, decay):
    n = jnp.arange(L, dtype=F32)
    t = n / max(L - 1, 1)
    w = 2.0 * math.pi * n / L
    f = jnp.linspace(1e-4, HY_BANDS - 1, HY_BANDS, dtype=F32)
    fw = w[:, None] * f[None, :]
    emb = jnp.concatenate([t[:, None], jnp.cos(fw), -jnp.sin(fw)], axis=-1)
    h = jnp.sin(freq[0].astype(F32) * (emb @ w1.astype(F32) + b1.astype(F32)))
    h = jnp.sin(freq[1].astype(F32) * (h @ w2.astype(F32) + b2.astype(F32)))
    h = (h @ w3.astype(F32)) * jnp.exp(-t[:, None] * jnp.abs(decay.astype(F32)))
    h = h.reshape(L, HY_ORDER, 2, HY_W)
    hf, hb = h[:, :, 0], h[:, :, 1]
    k = jnp.concatenate([hf, jnp.zeros((1, HY_ORDER, HY_W), F32), hb[1:][::-1]], axis=0)
    k = k / jnp.sum(jnp.abs(k), axis=0, keepdims=True)
    return jnp.fft.rfft(k, axis=0)


def fft_conv(z, kf):
    L = z.shape[1]
    zf = jnp.fft.rfft(z.astype(F32), n=2 * L, axis=1)
    y = jnp.fft.irfft(zf * kf[None], n=2 * L, axis=1)[:, :L]
    return y.astype(z.dtype)


def hyena_branch(u, sc_w, sc_b, w1, b1, w2, b2, freq, w3, decay, bias, w_o):
    L = u.shape[1]
    u = depthwise_conv(u, sc_w, sc_b)
    x1, x2, z = jnp.split(u, 3, axis=-1)
    kf = hyena_filter_fft(L, w1, b1, freq, w2, b2, w3, decay)
    for n, gate in enumerate((x1, x2)):
        z = gate * (fft_conv(z, kf[:, n]) + bias[n].astype(z.dtype) * z)
    return z @ w_o


def gated_merge(gate_logits, a, b, c, w_out):
    B, L, _ = gate_logits.shape
    g = jax.nn.sigmoid(gate_logits.astype(F32)).reshape(B, L, N_BRANCH, D_MODEL).astype(a.dtype)
    return (g[:, :, 0] * a + g[:, :, 1] * b + g[:, :, 2] * c) @ w_out


def token_mixer(hl, hc, rope_tab, lam, lam_init, w_in, attn_subln_g, w_attn_o, conf_dw_w, conf_dw_b,
                conf_ln_g, conf_ln_b, w_conf_o, hy_sc_w, hy_sc_b, hy_w1, hy_b1, hy_w2, hy_b2, hy_freq,
                hy_w3, hy_decay, hy_bias, w_hy_o, w_out, need_ctx):
    B, S, _ = hl.shape
    Lc = hc.shape[1]
    q_l, k_l, v_l, glu_l, hy_l, gate_l = jnp.split(hl @ w_in, SPLIT_POINTS, axis=-1)
    if need_ctx:
        q_c, k_c, v_c, glu_c, hy_c, gate_c = jnp.split(hc @ w_in, SPLIT_POINTS, axis=-1)
    else:
        k_c, v_c = jnp.split(hc @ w_in[:, SPLIT_POINTS[0]:SPLIT_POINTS[2]], 2, axis=-1)
    k_c = k_c.reshape(B, Lc, ATT_HEADS, 2, ATT_DQK)
    v_c = v_c.reshape(B, Lc, ATT_HEADS, ATT_DV)
    q_l = apply_axial_rope(q_l.reshape(B, S, ATT_HEADS, 2, ATT_DQK), rope_tab)
    k_l = apply_axial_rope(k_l.reshape(B, S, ATT_HEADS, 2, ATT_DQK), rope_tab)
    k_all = jnp.concatenate([k_c, k_l], axis=1)
    v_all = jnp.concatenate([v_c, v_l.reshape(B, S, ATT_HEADS, ATT_DV)], axis=1)
    att_l = diff_attn_out(blocked_diff_attention(q_l, k_all, v_all, lam), attn_subln_g, lam_init, w_attn_o)
    conf_l = conformer_branch(glu_l, conf_dw_w, conf_dw_b, conf_ln_g, conf_ln_b, w_conf_o)
    hyena_l = hyena_branch(hy_l, hy_sc_w, hy_sc_b, hy_w1, hy_b1, hy_w2, hy_b2, hy_freq, hy_w3, hy_decay,
                           hy_bias, w_hy_o)
    mix_l = gated_merge(gate_l, att_l, conf_l, hyena_l, w_out)
    if not need_ctx:
        return mix_l, None
    q_c = q_c.reshape(B, Lc, ATT_HEADS, 2, ATT_DQK)
    att_c = diff_attn_out(diff_attention(q_c, k_c, v_c, lam), attn_subln_g, lam_init, w_attn_o)
    conf_c = conformer_branch(glu_c, conf_dw_w, conf_dw_b, conf_ln_g, conf_ln_b, w_conf_o)
    hyena_c = hyena_branch(hy_c, hy_sc_w, hy_sc_b, hy_w1, hy_b1, hy_w2, hy_b2, hy_freq, hy_w3, hy_decay,
                           hy_bias, w_hy_o)
    mix_c = gated_merge(gate_c, att_c, conf_c, hyena_c, w_out)
    return mix_l, mix_c


def hier_moe(h, w_rg, w_re, w_gate, w_up, w_down):
    T, D = h.shape
    lg = (h @ w_rg).astype(F32)
    pg = jax.nn.softmax(lg, axis=-1)
    _, g_idx = lax.top_k(lg, 1)
    g_idx = g_idx[:, 0]
    p_grp = jnp.take_along_axis(pg, g_idx[:, None], axis=1)[:, 0]
    le = jnp.einsum('td,gde->tge', h, w_re).astype(F32)
    le = jnp.take_along_axis(le, g_idx[:, None, None], axis=1)[:, 0]
    e_val, e_idx = lax.top_k(le, TOP_K)
    wts = (jax.nn.softmax(e_val, axis=-1) * p_grp[:, None]).reshape(-1)
    eid = (g_idx[:, None] * E_PER_GROUP + e_idx).reshape(-1)
    tok = jnp.repeat(jnp.arange(T, dtype=jnp.int32), TOP_K)
    order = jnp.argsort(eid)
    e_s, t_s, w_s = eid[order], tok[order], wts[order]
    counts = jnp.bincount(eid, length=N_EXPERTS).astype(jnp.int32)
    pcounts = (counts + MOE_BLOCK - 1) // MOE_BLOCK * MOE_BLOCK
    pend = jnp.cumsum(pcounts)
    pstart = pend - pcounts
    cstart = jnp.cumsum(counts) - counts
    A = T * TOP_K
    dest = pstart[e_s] + (jnp.arange(A, dtype=jnp.int32) - cstart[e_s])
    n_blocks = -(-A // MOE_BLOCK) + N_EXPERTS
    P = n_blocks * MOE_BLOCK
    buf_tok = jnp.full((P,), T, jnp.int32).at[dest].set(t_s)
    buf_w = jnp.zeros((P,), F32).at[dest].set(w_s)
    blk_e = jnp.minimum(jnp.searchsorted(pend, jnp.arange(n_blocks, dtype=jnp.int32) * MOE_BLOCK, side='right'),
                        N_EXPERTS - 1)
    h_pad = jnp.concatenate([h, jnp.zeros((1, D), h.dtype)], axis=0)

    def run_block(args):
        idx, e = args
        xb = h_pad[idx]
        return (jax.nn.silu(xb @ w_gate[e]) * (xb @ w_up[e])) @ w_down[e]

    y = lax.map(run_block, (buf_tok.reshape(n_blocks, MOE_BLOCK), blk_e)).reshape(P, D)
    y = y * buf_w[:, None].astype(y.dtype)
    return jax.ops.segment_sum(y, buf_tok, num_segments=T + 1)[:T]


def setup_inputs(seed: int = 0) -> dict:
    key = jax.random.key(seed)
    ks = iter(jax.random.split(key, 64))
    D = D_MODEL

    def nrm(shape, scale):
        return jax.random.normal(next(ks), shape, F32) * scale

    return {
        'x': nrm((BATCH, SEQ, D), 1.0),
        'c': nrm((BATCH, D), 1.0),
        'ctx': nrm((BATCH, CTX_LEN, D), 1.0),
        'c_ctx': nrm((D,), 1.0),
        'w_mod': nrm((DEPTH, D, 6 * D), 0.5 * D ** -0.5),
        'b_mod': nrm((DEPTH, 6 * D), 0.02),
        'norm1_g': 1.0 + nrm((DEPTH, D), 0.02),
        'norm2_g': 1.0 + nrm((DEPTH, D), 0.02),
        'w_in': nrm((DEPTH, D, IN_COLS), D ** -0.5),
        'lam_q1': nrm((DEPTH, ATT_DQK), 0.1),
        'lam_k1': nrm((DEPTH, ATT_DQK), 0.1),
        'lam_q2': nrm((DEPTH, ATT_DQK), 0.1),
        'lam_k2': nrm((DEPTH, ATT_DQK), 0.1),
        'attn_subln_g': 1.0 + nrm((DEPTH, ATT_DV), 0.02),
        'w_attn_o': nrm((DEPTH, ATT_W, D), ATT_W ** -0.5),
        'conf_dw_w': nrm((DEPTH, CONF_K, CONF_W), CONF_K ** -0.5),
        'conf_dw_b': nrm((DEPTH, CONF_W), 0.02),
        'conf_ln_g': 1.0 + nrm((DEPTH, CONF_W), 0.02),
        'conf_ln_b': nrm((DEPTH, CONF_W), 0.02),
        'w_conf_o': nrm((DEPTH, CONF_W, D), CONF_W ** -0.5),
        'hy_sc_w': nrm((DEPTH, HY_SHORT_K, (HY_ORDER + 1) * HY_W), HY_SHORT_K ** -0.5),
        'hy_sc_b': nrm((DEPTH, (HY_ORDER + 1) * HY_W), 0.02),
        'hy_w1': nrm((DEPTH, HY_EMB, HY_FFN), HY_EMB ** -0.5),
        'hy_b1': nrm((DEPTH, HY_FFN), 0.02),
        'hy_w2': nrm((DEPTH, HY_FFN, HY_FFN), HY_FFN ** -0.5),
        'hy_b2': nrm((DEPTH, HY_FFN), 0.02),
        'hy_freq': 1.0 + nrm((DEPTH, 2, HY_FFN), 0.1),
        'hy_w3': nrm((DEPTH, HY_FFN, HY_FILT), HY_FFN ** -0.5),
        'hy_decay': jax.random.uniform(next(ks), (DEPTH, HY_FILT), F32, minval=3.0, maxval=15.0),
        'hy_bias': nrm((DEPTH, HY_ORDER, HY_W), 0.1),
        'w_hy_o': nrm((DEPTH, HY_W, D), HY_W ** -0.5),
        'w_out': nrm((DEPTH, D, D), D ** -0.5),
        'w_router_group': nrm((DEPTH, D, N_GROUPS), D ** -0.5),
        'w_router_expert': nrm((DEPTH, N_GROUPS, D, E_PER_GROUP), D ** -0.5),
        'w_exp_gate': nrm((DEPTH, N_EXPERTS, D, D_EXPERT), D ** -0.5),
        'w_exp_up': nrm((DEPTH, N_EXPERTS, D, D_EXPERT), D ** -0.5),
        'w_exp_down': nrm((DEPTH, N_EXPERTS, D_EXPERT, D), D_EXPERT ** -0.5),
        'norm_f_g': 1.0 + nrm((D,), 0.02),
    }


def reference(x, c, ctx, c_ctx, w_mod, b_mod, norm1_g, norm2_g, w_in, lam_q1, lam_k1, lam_q2, lam_k2,
              attn_subln_g, w_attn_o, conf_dw_w, conf_dw_b, conf_ln_g, conf_ln_b, w_conf_o, hy_sc_w, hy_sc_b,
              hy_w1, hy_b1, hy_w2, hy_b2, hy_freq, hy_w3, hy_decay, hy_bias, w_hy_o, w_out, w_router_group,
              w_router_expert, w_exp_gate, w_exp_up, w_exp_down, norm_f_g):
    B, S, D = x.shape
    Lc = ctx.shape[1]
    ROWS = S // GRID_W
    rows = jnp.repeat(jnp.arange(ROWS, dtype=jnp.int32), GRID_W)
    cols = jnp.tile(jnp.arange(GRID_W, dtype=jnp.int32), ROWS)
    rope_tab = axial_rope_tables(rows, cols)
    s_c = jax.nn.silu(c)
    s_cc = jax.nn.silu(c_ctx)
    x_l, x_c = x, ctx
    for l in range(DEPTH):
        need_ctx = l < DEPTH - 1
        lam_init = 0.8 - 0.6 * math.exp(-0.3 * l)
        mod_l = (s_c @ w_mod[l] + b_mod[l])[:, None, :]
        mod_c = s_cc @ w_mod[l] + b_mod[l]
        sh1, sc1, g1, sh2, sc2, g2 = jnp.split(mod_l, 6, axis=-1)
        csh1, csc1, cg1, csh2, csc2, cg2 = jnp.split(mod_c, 6, axis=-1)
        lam = (jnp.exp(jnp.sum(lam_q1[l].astype(F32) * lam_k1[l].astype(F32)))
               - jnp.exp(jnp.sum(lam_q2[l].astype(F32) * lam_k2[l].astype(F32))) + lam_init)
        hl = ada_norm(x_l, norm1_g[l], sh1, sc1)
        hc = ada_norm(x_c, norm1_g[l], csh1, csc1)
        mix_l, mix_c = token_mixer(hl, hc, rope_tab, lam, lam_init, w_in[l], attn_subln_g[l], w_attn_o[l],
                                   conf_dw_w[l], conf_dw_b[l], conf_ln_g[l], conf_ln_b[l], w_conf_o[l],
                                   hy_sc_w[l], hy_sc_b[l], hy_w1[l], hy_b1[l], hy_w2[l], hy_b2[l], hy_freq[l],
                                   hy_w3[l], hy_decay[l], hy_bias[l], w_hy_o[l], w_out[l], need_ctx)
        x_l = x_l + g1 * mix_l
        hl2 = ada_norm(x_l, norm2_g[l], sh2, sc2)
        if need_ctx:
            x_c = x_c + cg1 * mix_c
            hc2 = ada_norm(x_c, norm2_g[l], csh2, csc2)
            tokens = jnp.concatenate([hc2.reshape(-1, D), hl2.reshape(-1, D)], axis=0)
            y = hier_moe(tokens, w_router_group[l], w_router_expert[l], w_exp_gate[l], w_exp_up[l], w_exp_down[l])
            x_c = x_c + cg2 * y[:B * Lc].reshape(B, Lc, D)
            y_l = y[B * Lc:]
        else:
            y_l = hier_moe(hl2.reshape(-1, D), w_router_group[l], w_router_expert[l], w_exp_gate[l],
                           w_exp_up[l], w_exp_down[l])
        x_l = x_l + g2 * y_l.reshape(B, S, D)
    return rms_norm(x_l, norm_f_g)
```

```python
import functools
import math

import jax
import jax.numpy as jnp
from jax import lax
from jax.experimental import pallas as pl
from jax.experimental.pallas import tpu as pltpu

F32 = jnp.float32
BF16 = jnp.bfloat16

GRID_W = 64
EPS = 1e-6
ATT_HEADS = 8
ATT_DQK = 64
ATT_DV = 2 * ATT_DQK
ROPE_BASE = 10000.0
CONF_K = 31
HY_ORDER = 2
HY_SHORT_K = 3
HY_EMB = 33
HY_BANDS = (HY_EMB - 1) // 2
N_BRANCH = 3
N_GROUPS = 8
E_PER_GROUP = 8
N_EXPERTS = N_GROUPS * E_PER_GROUP

LANES = 128
SUBLANES = 8
VMEM_LIMIT_BYTES = 56 * 1024 * 1024

CONF_HALO = 16
HY_HALO = 8
MOE_BLOCK = 256
MOD_ROWS = 8


def _params(*sem):
    return pltpu.CompilerParams(dimension_semantics=sem, vmem_limit_bytes=VMEM_LIMIT_BYTES)


def _largest_tile(cands, *dims):
    for c in cands:
        if all(d % c == 0 for d in dims):
            return c
    raise ValueError(f"no tile in {cands} divides {dims}")


def _dot(a, b):
    return jnp.dot(a, b, preferred_element_type=F32)


def _split_bf16(x):
    hi = x.astype(BF16)
    lo = (x - hi.astype(F32)).astype(BF16)
    return hi, lo


def _dot3(a, b):
    ah, al = _split_bf16(a)
    bh, bl = _split_bf16(b)
    return _dot(ah, bh) + _dot(al, bh) + _dot(ah, bl)


def _mod_kernel(c_ref, w_ref, b_ref, o_ref):
    c = c_ref[...]
    s = c * jax.nn.sigmoid(c)
    o_ref[...] = _dot3(s, w_ref[...]) + b_ref[...]


def _modulation(cond, w_mod, b_mod):
    depth, d, n = w_mod.shape
    tn = _largest_tile((512, 256, 128), n)
    return pl.pallas_call(
        _mod_kernel,
        out_shape=jax.ShapeDtypeStruct((depth, MOD_ROWS, n), F32),
        grid=(depth, n // tn),
        in_specs=[
            pl.BlockSpec((MOD_ROWS, d), lambda l, j: (0, 0)),
            pl.BlockSpec((None, d, tn), lambda l, j: (l, 0, j)),
            pl.BlockSpec((None, 1, tn), lambda l, j: (l, 0, j)),
        ],
        out_specs=pl.BlockSpec((None, MOD_ROWS, tn), lambda l, j: (l, 0, j)),
        compiler_params=_params("parallel", "parallel"),
        name="modulation",
    )(cond, w_mod, b_mod.reshape(depth, 1, n))


def _ada_norm_value(x, g, shift, scale):
    ms = jnp.mean(x * x, axis=-1, keepdims=True)
    y = x * lax.rsqrt(ms + EPS) * g
    return y * (1.0 + scale) + shift


def _adanorm_kernel(x_ref, g_ref, sh_ref, sc_ref, o_ref):
    o_ref[...] = _ada_norm_value(x_ref[...], g_ref[...], sh_ref[...], sc_ref[...]).astype(o_ref.dtype)


def _mod_row_map(layer, tm, n_lat, s, b, col):
    def index_map(i, *_):
        row = jnp.where(i < n_lat, (i * tm) // s, b)
        return (layer * MOD_ROWS + row, 0, col)
    return index_map


def _adanorm(x_all, norm_g, mod3, layer, dims, tm):
    r, d = x_all.shape
    n_lat = dims["T"] // tm
    mm = functools.partial(_mod_row_map, layer, tm, n_lat, dims["S"], dims["B"])
    return pl.pallas_call(
        _adanorm_kernel,
        out_shape=jax.ShapeDtypeStruct((r, d), BF16),
        grid=(r // tm,),
        in_specs=[
            pl.BlockSpec((tm, d), lambda i: (i, 0)),
            pl.BlockSpec((None, 1, d), lambda i: (layer, 0, 0)),
            pl.BlockSpec((None, 1, d), mm(0)),
            pl.BlockSpec((None, 1, d), mm(1)),
        ],
        out_specs=pl.BlockSpec((tm, d), lambda i: (i, 0)),
        compiler_params=_params("parallel"),
        name="adanorm1",
    )(x_all, norm_g.reshape(norm_g.shape[0], 1, d), mod3, mod3)


def _qkv_kernel(a_ref, w_ref, tab_ref, o_ref, *, n_lat, tn, qk_cols):
    i = pl.program_id(0)
    j = pl.program_id(1)
    acc = _dot(a_ref[...], w_ref[...].astype(BF16))
    col0 = j * tn
    scale = jnp.where(col0 < qk_cols, ATT_DQK ** -0.5, 1.0).astype(F32)
    do_rope = jnp.logical_and(i < n_lat, col0 < 2 * qk_cols)

    @pl.when(do_rope)
    def _():
        c = tab_ref[:, 0:LANES]
        s1 = tab_ref[:, LANES:2 * LANES]
        s2 = tab_ref[:, 2 * LANES:3 * LANES]
        for h in range(tn // LANES):
            xh = acc[:, h * LANES:(h + 1) * LANES] * scale
            rot = xh * c + pltpu.roll(xh, LANES - 16, 1) * s1 + pltpu.roll(xh, 16, 1) * s2
            o_ref[:, h * LANES:(h + 1) * LANES] = rot.astype(o_ref.dtype)

    @pl.when(jnp.logical_not(do_rope))
    def _():
        o_ref[...] = (acc * scale).astype(o_ref.dtype)


def _plain_mm_kernel(a_ref, w_ref, o_ref):
    o_ref[...] = _dot(a_ref[...], w_ref[...].astype(BF16)).astype(o_ref.dtype)


def _sigmoid_mm_kernel(a_ref, w_ref, o_ref):
    o_ref[...] = jax.nn.sigmoid(_dot(a_ref[...], w_ref[...].astype(BF16))).astype(o_ref.dtype)


def _in_proj(hl, w_in, layer, rope_tab, dims, tm, tn):
    r, d = hl.shape
    qk, aw, cw, hw = dims["QK"], dims["AW"], dims["CW"], dims["HW"]
    n_lat = dims["T"] // tm
    s_tiles = dims["S"] // tm
    a_spec = pl.BlockSpec((tm, d), lambda i, j: (i, 0))

    def w_spec(col_off):
        off = col_off // tn
        return pl.BlockSpec((None, d, tn), lambda i, j: (layer, 0, j + off))

    def call(kernel, col_off, n_cols, dtype, name, extra_in=(), extra_specs=()):
        return pl.pallas_call(
            kernel,
            out_shape=jax.ShapeDtypeStruct((r, n_cols), dtype),
            grid=(r // tm, n_cols // tn),
            in_specs=[a_spec, w_spec(col_off), *extra_specs],
            out_specs=pl.BlockSpec((tm, tn), lambda i, j: (i, j)),
            compiler_params=_params("parallel", "arbitrary"),
            name=name,
        )(hl, w_in, *extra_in)

    qkv = call(
        functools.partial(_qkv_kernel, n_lat=n_lat, tn=tn, qk_cols=qk), 0, 2 * qk + aw, BF16, "in_proj_qkv",
        extra_in=(rope_tab,),
        extra_specs=(pl.BlockSpec((tm, 3 * LANES), lambda i, j: (i % s_tiles, 0)),))
    conv_in = call(_plain_mm_kernel, 2 * qk + aw, 2 * cw + 3 * hw, F32, "in_proj_conv")
    gates = call(_sigmoid_mm_kernel, 2 * qk + aw + 2 * cw + 3 * hw, N_BRANCH * d, BF16, "in_proj_gates")
    return qkv, conv_in, gates


def _attn_kernel(lam_ref, q_ref, g_ref, kc_ref, vc_ref, *rest, n_lat_chunks, tk, out_scale):
    if n_lat_chunks:
        kl_ref, vl_ref = rest[0], rest[1]
        rest = rest[2:]
    o_ref = rest[-1]
    q = q_ref[...]
    lane = lax.broadcasted_iota(jnp.int32, q.shape, 1)
    zero = jnp.zeros_like(q)
    q1 = jnp.where(lane < ATT_DQK, q, zero)
    q2 = jnp.where(lane >= ATT_DQK, q, zero)
    tq = q.shape[0]
    nt = (((1,), (1,)), ((), ()))

    def one_map(qm, k, v, m, l, acc):
        s = lax.dot_general(qm, k, nt, preferred_element_type=F32)
        m_new = jnp.maximum(m, jnp.max(s, axis=-1, keepdims=True))
        alpha = jnp.exp(m - m_new)
        p = jnp.exp(s - m_new)
        l_new = alpha * l + jnp.sum(p, axis=-1, keepdims=True)
        acc_new = alpha * acc + _dot(p.astype(BF16), v)
        return m_new, l_new, acc_new

    def step(k, v, carry):
        m1, l1, a1, m2, l2, a2 = carry
        m1, l1, a1 = one_map(q1, k, v, m1, l1, a1)
        m2, l2, a2 = one_map(q2, k, v, m2, l2, a2)
        return m1, l1, a1, m2, l2, a2

    neg = jnp.full((tq, 1), -jnp.inf, F32)
    z1 = jnp.zeros((tq, 1), F32)
    za = jnp.zeros((tq, ATT_DV), F32)
    carry = step(kc_ref[...], vc_ref[...], (neg, z1, za, neg, z1, za))
    if n_lat_chunks:
        def body(c, carry):
            off = pl.multiple_of(c * tk, tk)
            return step(kl_ref[pl.ds(off, tk), :], vl_ref[pl.ds(off, tk), :], carry)
        carry = lax.fori_loop(0, n_lat_chunks, body, carry)
    m1, l1, a1, m2, l2, a2 = carry
    o = a1 / l1 - lam_ref[0] * (a2 / l2)
    ms = jnp.mean(o * o, axis=-1, keepdims=True)
    o = o * lax.rsqrt(ms + 1e-5) * g_ref[...] * out_scale
    o_ref[...] = o.astype(o_ref.dtype)


def _attention(qkv, lam, subln_g, layer, lam_init, dims, with_ctx):
    b, s, lc, t = dims["B"], dims["S"], dims["Lc"], dims["T"]
    h = ATT_HEADS
    tq = _largest_tile((512, 256, 128), s)
    tk = _largest_tile((512, 256, 128), s)
    tqc = _largest_tile((512, 256, 128), lc)
    smem = pl.BlockSpec(memory_space=pltpu.SMEM)
    g3 = subln_g.reshape(subln_g.shape[0], 1, ATT_DV)
    g_spec = pl.BlockSpec((None, 1, ATT_DV), lambda bi, hi, ti: (layer, 0, 0))
    kc_spec = pl.BlockSpec((lc, LANES), lambda bi, hi, ti: (t // lc + bi, h + hi))
    vc_spec = pl.BlockSpec((lc, LANES), lambda bi, hi, ti: (t // lc + bi, 2 * h + hi))
    out_scale = 1.0 - lam_init

    att = pl.pallas_call(
        functools.partial(_attn_kernel, n_lat_chunks=s // tk, tk=tk, out_scale=out_scale),
        out_shape=jax.ShapeDtypeStruct((t, h * ATT_DV), BF16),
        grid=(b, h, s // tq),
        in_specs=[
            smem,
            pl.BlockSpec((tq, LANES), lambda bi, hi, ti: (bi * (s // tq) + ti, hi)),
            g_spec, kc_spec, vc_spec,
            pl.BlockSpec((s, LANES), lambda bi, hi, ti: (bi, h + hi)),
            pl.BlockSpec((s, LANES), lambda bi, hi, ti: (bi, 2 * h + hi)),
        ],
        out_specs=pl.BlockSpec((tq, LANES), lambda bi, hi, ti: (bi * (s // tq) + ti, hi)),
        compiler_params=_params("parallel", "parallel", "arbitrary"),
        name="diff_attn_latent",
    )(lam, qkv, g3, qkv, qkv, qkv, qkv)
    if not with_ctx:
        return att, att

    att_ctx = pl.pallas_call(
        functools.partial(_attn_kernel, n_lat_chunks=0, tk=tk, out_scale=out_scale),
        out_shape=jax.ShapeDtypeStruct((b * lc, h * ATT_DV), BF16),
        grid=(b, h, lc // tqc),
        in_specs=[
            smem,
            pl.BlockSpec((tqc, LANES), lambda bi, hi, ti: ((t + bi * lc) // tqc + ti, hi)),
            g_spec, kc_spec, vc_spec,
        ],
        out_specs=pl.BlockSpec((tqc, LANES), lambda bi, hi, ti: (bi * (lc // tqc) + ti, hi)),
        compiler_params=_params("parallel", "parallel", "arbitrary"),
        name="diff_attn_ctx",
    )(lam, qkv, g3, qkv, qkv)
    return att, att_ctx


def _seq_edges(i, n_lat, tps_lat, tps_ctx):
    is_lat = i < n_lat
    pos = jnp.where(is_lat, i % tps_lat, (i - n_lat) % tps_ctx)
    n = jnp.where(is_lat, tps_lat, tps_ctx)
    return pos == 0, pos == n - 1


def _halo_specs(ts, halo, width, col, n_rows):
    per = ts // halo
    last = n_rows // halo - 1
    prev = pl.BlockSpec((halo, width), lambda i, *_: (jnp.maximum(i * per - 1, 0), col(*_)))
    cur = pl.BlockSpec((ts, width), lambda i, *_: (i, col(*_)))
    nxt = pl.BlockSpec((halo, width), lambda i, *_: (jnp.minimum((i + 1) * per, last), col(*_)))
    return prev, cur, nxt


def _conf_kernel(ap_ref, ac_ref, an_ref, gp_ref, gc_ref, gn_ref, w_ref, b_ref, lg_ref, lb_ref, o_ref, ybuf,
                 *, ts, n_lat, tps_lat, tps_ctx, row_chunk):
    first, last = _seq_edges(pl.program_id(0), n_lat, tps_lat, tps_ctx)
    halo = CONF_HALO

    def glu(a, g):
        return a * jax.nn.sigmoid(g)

    prev = glu(ap_ref[...], gp_ref[...])
    nxt = glu(an_ref[...], gn_ref[...])
    ybuf[0:halo, :] = jnp.where(first, jnp.zeros_like(prev), prev)
    ybuf[halo:halo + ts, :] = glu(ac_ref[...], gc_ref[...])
    ybuf[halo + ts:, :] = jnp.where(last, jnp.zeros_like(nxt), nxt)
    base = halo - CONF_K // 2
    for r0 in range(0, ts, row_chunk):
        acc = jnp.zeros((row_chunk, ybuf.shape[1]), F32) + b_ref[...]
        for j in range(CONF_K):
            acc = acc + w_ref[j:j + 1, :] * ybuf[r0 + base + j:r0 + base + j + row_chunk, :]
        mu = jnp.mean(acc, axis=-1, keepdims=True)
        xc = acc - mu
        var = jnp.mean(xc * xc, axis=-1, keepdims=True)
        y = xc * lax.rsqrt(var + 1e-5) * lg_ref[...] + lb_ref[...]
        o_ref[r0:r0 + row_chunk, :] = (y * jax.nn.sigmoid(y)).astype(o_ref.dtype)


def _conformer(conv_in, dw_w, dw_b, ln_g, ln_b, layer, dims, n_rows, ts):
    cw = dims["CW"]
    depth = dw_w.shape[0]
    n_lat = dims["T"] // ts
    a_specs = _halo_specs(ts, CONF_HALO, cw, lambda: 0, conv_in.shape[0])
    g_specs = _halo_specs(ts, CONF_HALO, cw, lambda: 1, conv_in.shape[0])
    vec = lambda: pl.BlockSpec((None, 1, cw), lambda i: (layer, 0, 0))
    kern = functools.partial(_conf_kernel, ts=ts, n_lat=n_lat, tps_lat=dims["S"] // ts, tps_ctx=dims["Lc"] // ts,
                             row_chunk=min(32, ts))
    return pl.pallas_call(
        kern,
        out_shape=jax.ShapeDtypeStruct((n_rows, cw), BF16),
        grid=(n_rows // ts,),
        in_specs=[*a_specs, *g_specs,
                  pl.BlockSpec((None, CONF_K, cw), lambda i: (layer, 0, 0)), vec(), vec(), vec()],
        out_specs=pl.BlockSpec((ts, cw), lambda i: (i, 0)),
        scratch_shapes=[pltpu.VMEM((ts + 2 * CONF_HALO, cw), F32)],
        compiler_params=_params("parallel"),
        name="conformer_conv",
    )(conv_in, conv_in, conv_in, conv_in, conv_in, conv_in, dw_w,
      dw_b.reshape(depth, 1, cw), ln_g.reshape(depth, 1, cw), ln_b.reshape(depth, 1, cw))


def _hy_short_kernel(p_ref, c_ref, n_ref, w_ref, b_ref, o_ref, ybuf, *, ts, n_lat, tps_lat, tps_ctx):
    first, last = _seq_edges(pl.program_id(0), n_lat, tps_lat, tps_ctx)
    halo = HY_HALO
    prev = p_ref[...]
    nxt = n_ref[...]
    ybuf[0:halo, :] = jnp.where(first, jnp.zeros_like(prev), prev)
    ybuf[halo:halo + ts, :] = c_ref[...]
    ybuf[halo + ts:, :] = jnp.where(last, jnp.zeros_like(nxt), nxt)
    acc = b_ref[...] + w_ref[0:1, :] * ybuf[halo - 1:halo - 1 + ts, :]
    acc = acc + w_ref[1:2, :] * ybuf[halo:halo + ts, :]
    acc = acc + w_ref[2:3, :] * ybuf[halo + 1:halo + 1 + ts, :]
    o_ref[...] = acc


def _hyena_short(conv_in, sc_w, sc_b, layer, dims, n_rows, ts):
    cw, hw = dims["CW"], dims["HW"]
    depth = sc_w.shape[0]
    n_lat = dims["T"] // ts
    col0 = 2 * cw // hw
    specs = _halo_specs(ts, HY_HALO, hw, lambda m: col0 + m, conv_in.shape[0])
    kern = functools.partial(_hy_short_kernel, ts=ts, n_lat=n_lat, tps_lat=dims["S"] // ts,
                             tps_ctx=dims["Lc"] // ts)
    return pl.pallas_call(
        kern,
        out_shape=jax.ShapeDtypeStruct((n_rows, 3 * hw), F32),
        grid=(n_rows // ts, 3),
        in_specs=[*specs,
                  pl.BlockSpec((None, HY_SHORT_K, hw), lambda i, m: (layer, 0, m)),
                  pl.BlockSpec((None, 1, hw), lambda i, m: (layer, 0, m))],
        out_specs=pl.BlockSpec((ts, hw), lambda i, m: (i, m)),
        scratch_shapes=[pltpu.VMEM((ts + 2 * HY_HALO, hw), F32)],
        compiler_params=_params("parallel", "arbitrary"),
        name="hyena_short_conv",
    )(conv_in, conv_in, conv_in, sc_w, sc_b.reshape(depth, 1, 3 * hw))


def _filter_kernel(emb_ref, w1_ref, b1_ref, f_ref, w2_ref, b2_ref, w3_ref, dec_ref, h_ref, asum_ref, *, hw, tl):
    i = pl.program_id(0)
    hp = lax.Precision.HIGHEST
    emb = emb_ref[...]
    t = emb[:, 0:1]
    h1 = jnp.sin(f_ref[0:1, :] * (jnp.dot(emb, w1_ref[...], precision=hp, preferred_element_type=F32) + b1_ref[...]))
    h2 = jnp.sin(f_ref[1:2, :] * (jnp.dot(h1, w2_ref[...], precision=hp, preferred_element_type=F32) + b2_ref[...]))
    h = jnp.dot(h2, w3_ref[...], precision=hp, preferred_element_type=F32)
    h = h * jnp.exp(-t * jnp.abs(dec_ref[...]))
    col = lax.broadcasted_iota(jnp.int32, h.shape, 1)
    row = lax.broadcasted_iota(jnp.int32, h.shape, 0) + i * tl
    backward = (col // hw) % 2 == 1
    h = jnp.where(jnp.logical_and(backward, row == 0), 0.0, h)
    h_ref[...] = h.astype(h_ref.dtype)

    @pl.when(i == 0)
    def _():
        asum_ref[...] = jnp.zeros_like(asum_ref)

    part = jnp.abs(h).reshape(tl // SUBLANES, SUBLANES, h.shape[1]).sum(axis=0)
    asum_ref[...] += part


def _hyena_filter(emb, w1, b1, freq, w2, b2, w3, decay, layer, hw):
    l, ke = emb.shape
    depth, _, ffn = w1.shape
    nf = w3.shape[2]
    tl = _largest_tile((512, 256, 128), l)
    w1p = jnp.pad(w1, ((0, 0), (0, ke - w1.shape[1]), (0, 0)))
    vec = lambda n: pl.BlockSpec((None, 1, n), lambda i: (layer, 0, 0))
    return pl.pallas_call(
        functools.partial(_filter_kernel, hw=hw, tl=tl),
        out_shape=(jax.ShapeDtypeStruct((l, nf), BF16), jax.ShapeDtypeStruct((SUBLANES, nf), F32)),
        grid=(l // tl,),
        in_specs=[
            pl.BlockSpec((tl, ke), lambda i: (i, 0)),
            pl.BlockSpec((None, ke, ffn), lambda i: (layer, 0, 0)),
            vec(ffn),
            pl.BlockSpec((None, 2, ffn), lambda i: (layer, 0, 0)),
            pl.BlockSpec((None, ffn, ffn), lambda i: (layer, 0, 0)),
            vec(ffn),
            pl.BlockSpec((None, ffn, nf), lambda i: (layer, 0, 0)),
            vec(nf),
        ],
        out_specs=(pl.BlockSpec((tl, nf), lambda i: (i, 0)), pl.BlockSpec((SUBLANES, nf), lambda i: (0, 0))),
        compiler_params=_params("arbitrary"),
        name="hyena_filter_mlp",
    )(emb, w1p, b1.reshape(depth, 1, ffn), freq, w2, b2.reshape(depth, 1, ffn), w3, decay.reshape(depth, 1, nf))


def _bin_scale(i, tm, n_bins):
    row = lax.broadcasted_iota(jnp.int32, (tm, 1), 0) + i * tm
    is0 = row == 0
    return is0, jnp.where(is0, 1.0 / (2 * n_bins), 2.0 / (2 * n_bins)).astype(F32)


def _filter_dft_kernel(c_ref, s_ref, hf_ref, hb_ref, af_ref, ab_ref, kre_ref, kim_ref, *, tm, n_bins):
    i = pl.program_id(1)
    c = c_ref[...]
    s = s_ref[...]
    fre, fim = _dot(c, hf_ref[...]), _dot(s, hf_ref[...])
    bre, bim = _dot(c, hb_ref[...]), _dot(s, hb_ref[...])
    norm = jnp.sum(af_ref[...], axis=0, keepdims=True) + jnp.sum(ab_ref[...], axis=0, keepdims=True)
    is0, cs = _bin_scale(i, tm, n_bins)
    w = cs / norm
    kre_ref[...] = (fre + bre) * w
    kim_ref[...] = jnp.where(is0, fim + bim, fim - bim) * w


def _filter_spectrum(h, asum, c_tab, s_tab, hw):
    l, nf = h.shape
    n_ord = nf // (2 * hw)
    tm = _largest_tile((512, 256, 128), l)
    out = jax.ShapeDtypeStruct((l, n_ord * hw), F32)
    tab = pl.BlockSpec((tm, l), lambda o, i: (i, 0))
    return pl.pallas_call(
        functools.partial(_filter_dft_kernel, tm=tm, n_bins=l),
        out_shape=(out, out),
        grid=(n_ord, l // tm),
        in_specs=[tab, tab,
                  pl.BlockSpec((l, hw), lambda o, i: (0, 2 * o)),
                  pl.BlockSpec((l, hw), lambda o, i: (0, 2 * o + 1)),
                  pl.BlockSpec((SUBLANES, hw), lambda o, i: (0, 2 * o)),
                  pl.BlockSpec((SUBLANES, hw), lambda o, i: (0, 2 * o + 1))],
        out_specs=(pl.BlockSpec((tm, hw), lambda o, i: (i, o)), pl.BlockSpec((tm, hw), lambda o, i: (i, o))),
        compiler_params=_params("parallel", "arbitrary"),
        name="hyena_filter_dft",
    )(c_tab, s_tab, h, h, asum, asum)


def _fwd_dft_kernel(c_ref, s_ref, z_ref, kre_ref, kim_ref, yre_ref, yim_ref, zb, *, tm):
    i = pl.program_id(1)

    @pl.when(i == 0)
    def _():
        zb[...] = z_ref[...].astype(BF16)

    xre = _dot(c_ref[...], zb[...])
    xim = _dot(s_ref[...], zb[...])
    kre, kim = kre_ref[...], kim_ref[...]
    is0 = (lax.broadcasted_iota(jnp.int32, (tm, 1), 0) + i * tm) == 0
    yre = xre * kre - jnp.where(is0, 0.0, xim * kim)
    yim = jnp.where(is0, xim * kim, xre * kim + xim * kre)
    yre_ref[...] = yre.astype(BF16)
    yim_ref[...] = yim.astype(BF16)


def _fwd_dft(z_arr, z_row0, z_col, kre, kim, order, c_tab, s_tab, b, l, hw):
    tm = _largest_tile((512, 256, 128), l)
    out = jax.ShapeDtypeStruct((b * l, hw), BF16)
    tab = pl.BlockSpec((tm, l), lambda bi, i: (i, 0))
    kspec = pl.BlockSpec((tm, hw), lambda bi, i: (i, order))
    ospec = pl.BlockSpec((tm, hw), lambda bi, i: (bi * (l // tm) + i, 0))
    return pl.pallas_call(
        functools.partial(_fwd_dft_kernel, tm=tm),
        out_shape=(out, out),
        grid=(b, l // tm),
        in_specs=[tab, tab, pl.BlockSpec((l, hw), lambda bi, i: (z_row0 // l + bi, z_col)), kspec, kspec],
        out_specs=(ospec, ospec),
        scratch_shapes=[pltpu.VMEM((l, hw), BF16)],
        compiler_params=_params("parallel", "arbitrary"),
        name="hyena_fwd_dft",
    )(c_tab, s_tab, z_arr, kre, kim)


def _inv_dft_kernel(c_ref, st_ref, yre_ref, yim_ref, gate_ref, z_ref, bias_ref, o_ref):
    y = _dot(c_ref[...], yre_ref[...]) + _dot(st_ref[...], yim_ref[...])
    o_ref[...] = (gate_ref[...] * (y + bias_ref[...] * z_ref[...])).astype(o_ref.dtype)


def _inv_dft(yre, yim, u, u_row0, gate_col, z_arr, z_row0, z_col, bias, layer, order, c_tab, st_tab, b, l, hw,
             out_dtype):
    tm = _largest_tile((512, 256, 128), l)
    per = l // tm
    tab = pl.BlockSpec((tm, l), lambda bi, i: (i, 0))
    yspec = pl.BlockSpec((l, hw), lambda bi, i: (bi, 0))
    return pl.pallas_call(
        _inv_dft_kernel,
        out_shape=jax.ShapeDtypeStruct((b * l, hw), out_dtype),
        grid=(b, per),
        in_specs=[tab, tab, yspec, yspec,
                  pl.BlockSpec((tm, hw), lambda bi, i: (u_row0 // tm + bi * per + i, gate_col)),
                  pl.BlockSpec((tm, hw), lambda bi, i: (z_row0 // tm + bi * per + i, z_col)),
                  pl.BlockSpec((None, None, 1, hw), lambda bi, i: (layer, order, 0, 0))],
        out_specs=pl.BlockSpec((tm, hw), lambda bi, i: (bi * per + i, 0)),
        compiler_params=_params("parallel", "arbitrary"),
        name="hyena_inv_dft",
    )(c_tab, st_tab, yre, yim, u, z_arr, bias.reshape(bias.shape[0], bias.shape[1], 1, hw))


def _dft_tables(l):
    n = 2 * l
    k = jnp.arange(l, dtype=jnp.int32)
    m = (k[:, None] * k[None, :]) % n
    ang = m.astype(F32) * (2.0 * math.pi / n)
    c = jnp.cos(ang)
    s = -jnp.sin(ang)
    alt = jnp.where(k % 2 == 0, 1.0, -1.0).astype(F32)
    s = jnp.where(k[:, None] == 0, alt[None, :], s)
    return c.astype(BF16), s.astype(BF16), s.T.astype(BF16)


def _hyena_embedding(l):
    n = jnp.arange(l, dtype=F32)
    t = n / max(l - 1, 1)
    w = 2.0 * math.pi * n / l
    f = jnp.linspace(1e-4, HY_BANDS - 1, HY_BANDS, dtype=F32)
    fw = w[:, None] * f[None, :]
    emb = jnp.concatenate([t[:, None], jnp.cos(fw), -jnp.sin(fw)], axis=-1)
    return jnp.pad(emb, ((0, 0), (0, LANES - HY_EMB)))


def _hyena_long(u, u_row0, b, l, layer, p, tables, emb, dims):
    hw = dims["HW"]
    c_tab, s_tab, st_tab = tables
    h, asum = _hyena_filter(emb, p["hy_w1"], p["hy_b1"], p["hy_freq"], p["hy_w2"], p["hy_b2"], p["hy_w3"],
                            p["hy_decay"], layer, hw)
    kre, kim = _filter_spectrum(h, asum, c_tab, s_tab, hw)
    yre, yim = _fwd_dft(u, u_row0, 2, kre, kim, 0, c_tab, s_tab, b, l, hw)
    z1 = _inv_dft(yre, yim, u, u_row0, 0, u, u_row0, 2, p["hy_bias"], layer, 0, c_tab, st_tab, b, l, hw, F32)
    yre, yim = _fwd_dft(z1, 0, 0, kre, kim, 1, c_tab, s_tab, b, l, hw)
    return _inv_dft(yre, yim, u, u_row0, 1, z1, 0, 0, p["hy_bias"], layer, 1, c_tab, st_tab, b, l, hw, BF16)


def _merge_kernel(attl_ref, attc_ref, conf_ref, hyl_ref, hyc_ref, wa_ref, wc_ref, wh_ref, g0_ref, g1_ref, g2_ref,
                  o_ref, *, n_lat):
    is_lat = pl.program_id(0) < n_lat
    a = _dot(jnp.where(is_lat, attl_ref[...], attc_ref[...]), wa_ref[...].astype(BF16))
    c = _dot(conf_ref[...], wc_ref[...].astype(BF16))
    h = _dot(jnp.where(is_lat, hyl_ref[...], hyc_ref[...]), wh_ref[...].astype(BF16))
    m = g0_ref[...].astype(F32) * a + g1_ref[...].astype(F32) * c + g2_ref[...].astype(F32) * h
    o_ref[...] = m.astype(o_ref.dtype)


def _merge(att, att_ctx, conf, hy, hy_ctx, gates, w_attn_o, w_conf_o, w_hy_o, layer, n_rows, n_lat_rows, d, tm, tn):
    aw, cw, hw = att.shape[1], conf.shape[1], hy.shape[1]
    gb = d // tn
    n_lat = n_lat_rows // tm
    lat_map = lambda i, j: (jnp.minimum(i, n_lat - 1), 0)
    ctx_map = lambda i, j: (jnp.maximum(i - n_lat, 0), 0)
    return pl.pallas_call(
        functools.partial(_merge_kernel, n_lat=n_lat),
        out_shape=jax.ShapeDtypeStruct((n_rows, d), BF16),
        grid=(n_rows // tm, d // tn),
        in_specs=[
            pl.BlockSpec((tm, aw), lat_map),
            pl.BlockSpec((tm, aw), ctx_map),
            pl.BlockSpec((tm, cw), lambda i, j: (i, 0)),
            pl.BlockSpec((tm, hw), lat_map),
            pl.BlockSpec((tm, hw), ctx_map),
            pl.BlockSpec((None, aw, tn), lambda i, j: (layer, 0, j)),
            pl.BlockSpec((None, cw, tn), lambda i, j: (layer, 0, j)),
            pl.BlockSpec((None, hw, tn), lambda i, j: (layer, 0, j)),
            pl.BlockSpec((tm, tn), lambda i, j: (i, j)),
            pl.BlockSpec((tm, tn), lambda i, j: (i, gb + j)),
            pl.BlockSpec((tm, tn), lambda i, j: (i, 2 * gb + j)),
        ],
        out_specs=pl.BlockSpec((tm, tn), lambda i, j: (i, j)),
        compiler_params=_params("parallel", "arbitrary"),
        name="branch_proj_merge",
    )(att, att_ctx, conf, hy, hy_ctx, w_attn_o, w_conf_o, w_hy_o, gates, gates, gates)


def _out_proj_kernel(a_ref, w_ref, x_ref, g_ref, o_ref):
    o_ref[...] = x_ref[...] + g_ref[...] * _dot(a_ref[...], w_ref[...].astype(BF16))


def _out_proj(merged, w_out, x_all, mod3, layer, n_rows, dims, tm, tn):
    d = x_all.shape[1]
    n_lat = dims["T"] // tm
    mm = functools.partial(_mod_row_map, layer, tm, n_lat, dims["S"], dims["B"])

    def gate_map(i, j):
        row, z, _ = mm(0)(i)
        return (row, z, 2 * (d // tn) + j)

    return pl.pallas_call(
        _out_proj_kernel,
        out_shape=jax.ShapeDtypeStruct((n_rows, d), F32),
        grid=(n_rows // tm, d // tn),
        in_specs=[
            pl.BlockSpec((tm, d), lambda i, j: (i, 0)),
            pl.BlockSpec((None, d, tn), lambda i, j: (layer, 0, j)),
            pl.BlockSpec((tm, tn), lambda i, j: (i, j)),
            pl.BlockSpec((None, 1, tn), gate_map),
        ],
        out_specs=pl.BlockSpec((tm, tn), lambda i, j: (i, j)),
        compiler_params=_params("parallel", "arbitrary"),
        name="out_proj_residual",
    )(merged, w_out, x_all, mod3)


def _router_kernel(x_ref, g_ref, sh_ref, sc_ref, wr_ref, h_ref, r_ref):
    h = _ada_norm_value(x_ref[...], g_ref[...], sh_ref[...], sc_ref[...])
    h_ref[...] = h
    logits = _dot3(h, wr_ref[...])
    lane = lax.broadcasted_iota(jnp.int32, logits.shape, 1).astype(F32)
    ninf = jnp.float32(-jnp.inf)
    big = jnp.float32(LANES)

    def first_argmax(v):
        mx = jnp.max(v, axis=-1, keepdims=True)
        idx = jnp.min(jnp.where(v == mx, lane, big), axis=-1, keepdims=True)
        return mx, idx

    lg = jnp.where(lane < N_GROUPS, logits, ninf)
    gmax, gidx = first_argmax(lg)
    p_grp = 1.0 / jnp.sum(jnp.exp(lg - gmax), axis=-1, keepdims=True)
    lo = N_GROUPS + gidx * E_PER_GROUP
    le = jnp.where(jnp.logical_and(lane >= lo, lane < lo + E_PER_GROUP), logits, ninf)
    e1, i1 = first_argmax(le)
    e2, i2 = first_argmax(jnp.where(lane == i1, ninf, le))
    t = jnp.exp(e2 - e1)
    w1 = p_grp / (1.0 + t)
    w2 = p_grp * t / (1.0 + t)
    out = jnp.where(lane == 0, i1 - N_GROUPS,
                    jnp.where(lane == 1, i2 - N_GROUPS,
                              jnp.where(lane == 2, w1, jnp.where(lane == 3, w2, 0.0))))
    r_ref[...] = out


def _router(x_all, norm_g, mod3, w_router, layer, n_rows, dims, tm):
    r, d = x_all.shape
    n_lat = dims["T"] // tm
    mm = functools.partial(_mod_row_map, layer, tm, n_lat, dims["S"], dims["B"])
    return pl.pallas_call(
        _router_kernel,
        out_shape=(jax.ShapeDtypeStruct((n_rows, d), F32), jax.ShapeDtypeStruct((n_rows, LANES), F32)),
        grid=(n_rows // tm,),
        in_specs=[
            pl.BlockSpec((tm, d), lambda i: (i, 0)),
            pl.BlockSpec((None, 1, d), lambda i: (layer, 0, 0)),
            pl.BlockSpec((None, 1, d), mm(3)),
            pl.BlockSpec((None, 1, d), mm(4)),
            pl.BlockSpec((d, LANES), lambda i: (0, 0)),
        ],
        out_specs=(pl.BlockSpec((tm, d), lambda i: (i, 0)), pl.BlockSpec((tm, LANES), lambda i: (i, 0))),
        compiler_params=_params("parallel"),
        name="adanorm2_router",
    )(x_all, norm_g.reshape(norm_g.shape[0], 1, d), mod3, mod3, w_router)


def _dispatch(route, n_tok, bm):
    eid = route[:, 0:2].astype(jnp.int32).reshape(-1)
    wts = route[:, 2:4].reshape(-1)
    a = 2 * n_tok
    order = jnp.argsort(eid)
    e_s = eid[order]
    counts = jnp.bincount(eid, length=N_EXPERTS).astype(jnp.int32)
    pcounts = (counts + bm - 1) // bm * bm
    pend = jnp.cumsum(pcounts)
    pstart = pend - pcounts
    cstart = jnp.cumsum(counts) - counts
    dest = pstart[e_s] + (jnp.arange(a, dtype=jnp.int32) - cstart[e_s])
    n_blocks = -(-a // bm) + N_EXPERTS
    p = n_blocks * bm
    buf_tok = jnp.zeros((p,), jnp.int32).at[dest].set((order // 2).astype(jnp.int32))
    buf_w = jnp.zeros((p,), F32).at[dest].set(wts[order])
    slot = jnp.zeros((a,), jnp.int32).at[order].set(dest)
    blk_e = jnp.minimum(jnp.searchsorted(pend, jnp.arange(n_blocks, dtype=jnp.int32) * bm, side='right'),
                        N_EXPERTS - 1).astype(jnp.int32)
    n_used = (pend[-1] // bm).astype(jnp.int32).reshape(1)
    return buf_tok, buf_w, slot, blk_e, n_used, n_blocks


def _row_copy(src_hbm, row, dst, dst_row, sem):
    return pltpu.make_async_copy(src_hbm.at[pl.ds(row, 1), :], dst.at[pl.ds(dst_row, 1), :], sem)


def _gather_kernel(nused_ref, idx_ref, src_hbm, o_ref, buf, sem, *, bm):
    i = pl.program_id(0)

    @pl.when(i < nused_ref[0])
    def _():
        def issue(r, carry):
            _row_copy(src_hbm, idx_ref[0, 0, r], buf, r, sem).start()
            return carry
        lax.fori_loop(0, bm, issue, 0)
        pltpu.make_async_copy(src_hbm.at[pl.ds(0, bm), :], buf, sem).wait()
        o_ref[...] = buf[...].astype(o_ref.dtype)

    @pl.when(i >= nused_ref[0])
    def _():
        o_ref[...] = jnp.zeros_like(o_ref)


def _gather_rows(h2, buf_tok, n_used, n_blocks, bm):
    d = h2.shape[1]
    return pl.pallas_call(
        functools.partial(_gather_kernel, bm=bm),
        out_shape=jax.ShapeDtypeStruct((n_blocks * bm, d), BF16),
        grid_spec=pltpu.PrefetchScalarGridSpec(
            num_scalar_prefetch=1,
            grid=(n_blocks,),
            in_specs=[pl.BlockSpec((1, 1, bm), lambda i, nu: (i, 0, 0), memory_space=pltpu.SMEM),
                      pl.BlockSpec(memory_space=pl.ANY)],
            out_specs=pl.BlockSpec((bm, d), lambda i, nu: (i, 0)),
            scratch_shapes=[pltpu.VMEM((bm, d), F32), pltpu.SemaphoreType.DMA],
        ),
        compiler_params=_params("arbitrary"),
        name="moe_gather",
    )(n_used, buf_tok.reshape(n_blocks, 1, bm), h2)


def _expert_kernel(be_ref, nused_ref, x_ref, wg_ref, wu_ref, wd_ref, w_ref, o_ref, wg_s, wu_s, wd_s):
    i = pl.program_id(0)
    changed = jnp.logical_or(i == 0, be_ref[i] != be_ref[jnp.maximum(i - 1, 0)])

    @pl.when(i < nused_ref[0])
    def _():
        @pl.when(changed)
        def _():
            wg_s[...] = wg_ref[...].astype(BF16)
            wu_s[...] = wu_ref[...].astype(BF16)
            wd_s[...] = wd_ref[...].astype(BF16)

        x = x_ref[...]
        g = _dot(x, wg_s[...])
        u = _dot(x, wu_s[...])
        mid = (g * jax.nn.sigmoid(g) * u).astype(BF16)
        o_ref[...] = _dot(mid, wd_s[...]) * w_ref[...]

    @pl.when(i >= nused_ref[0])
    def _():
        o_ref[...] = jnp.zeros_like(o_ref)


def _experts(xs, buf_w, blk_e, n_used, w_gate, w_up, w_down, layer, n_blocks, bm):
    d = xs.shape[1]
    f = w_gate.shape[3]
    return pl.pallas_call(
        _expert_kernel,
        out_shape=jax.ShapeDtypeStruct((n_blocks * bm, d), F32),
        grid_spec=pltpu.PrefetchScalarGridSpec(
            num_scalar_prefetch=2,
            grid=(n_blocks,),
            in_specs=[
                pl.BlockSpec((bm, d), lambda i, be, nu: (i, 0)),
                pl.BlockSpec((None, None, d, f), lambda i, be, nu: (layer, be[i], 0, 0)),
                pl.BlockSpec((None, None, d, f), lambda i, be, nu: (layer, be[i], 0, 0)),
                pl.BlockSpec((None, None, f, d), lambda i, be, nu: (layer, be[i], 0, 0)),
                pl.BlockSpec((bm, 1), lambda i, be, nu: (i, 0)),
            ],
            out_specs=pl.BlockSpec((bm, d), lambda i, be, nu: (i, 0)),
            scratch_shapes=[pltpu.VMEM((d, f), BF16), pltpu.VMEM((d, f), BF16), pltpu.VMEM((f, d), BF16)],
        ),
        compiler_params=_params("arbitrary"),
        name="moe_experts",
    )(blk_e, n_used, xs, w_gate, w_up, w_down, buf_w.reshape(-1, 1))


def _combine_kernel(slot_ref, y_hbm, x_ref, g_ref, *rest, tc, final):
    if final:
        ng_ref, o_ref, buf0, buf1, sem = rest
    else:
        o_ref, buf0, buf1, sem = rest

    def issue(r, carry):
        _row_copy(y_hbm, slot_ref[0, 0, 2 * r], buf0, r, sem).start()
        _row_copy(y_hbm, slot_ref[0, 0, 2 * r + 1], buf1, r, sem).start()
        return carry
    lax.fori_loop(0, tc, issue, 0)
    pltpu.make_async_copy(y_hbm.at[pl.ds(0, tc), :], buf0, sem).wait()
    pltpu.make_async_copy(y_hbm.at[pl.ds(0, tc), :], buf1, sem).wait()
    x = x_ref[...] + g_ref[...] * (buf0[...] + buf1[...])
    if final:
        ms = jnp.mean(x * x, axis=-1, keepdims=True)
        x = x * lax.rsqrt(ms + EPS) * ng_ref[...]
    o_ref[...] = x


def _combine(yb, slot, x_all, mod3, layer, n_rows, dims, tc, norm_f_g=None):
    r, d = x_all.shape
    n_lat = dims["T"] // tc
    mm = functools.partial(_mod_row_map, layer, tc, n_lat, dims["S"], dims["B"])
    final = norm_f_g is not None
    in_specs = [
        pl.BlockSpec((1, 1, 2 * tc), lambda i: (i, 0, 0), memory_space=pltpu.SMEM),
        pl.BlockSpec(memory_space=pl.ANY),
        pl.BlockSpec((tc, d), lambda i: (i, 0)),
        pl.BlockSpec((None, 1, d), mm(5)),
    ]
    args = [slot.reshape(n_rows // tc, 1, 2 * tc), yb, x_all, mod3]
    if final:
        in_specs.append(pl.BlockSpec((1, d), lambda i: (0, 0)))
        args.append(norm_f_g.reshape(1, d))
    return pl.pallas_call(
        functools.partial(_combine_kernel, tc=tc, final=final),
        out_shape=jax.ShapeDtypeStruct((n_rows, d), F32),
        grid=(n_rows // tc,),
        in_specs=in_specs,
        out_specs=pl.BlockSpec((tc, d), lambda i: (i, 0)),
        scratch_shapes=[pltpu.VMEM((tc, d), F32), pltpu.VMEM((tc, d), F32), pltpu.SemaphoreType.DMA],
        compiler_params=_params("arbitrary"),
        name="moe_combine_residual",
    )(*args)


def _rope_table(s):
    half = ATT_DQK // 2
    rows = jnp.repeat(jnp.arange(s // GRID_W, dtype=jnp.int32), GRID_W)
    cols = jnp.tile(jnp.arange(GRID_W, dtype=jnp.int32), s // GRID_W)
    inv = ROPE_BASE ** (-jnp.arange(0, half, 2, dtype=F32) / half)
    ar = rows.astype(F32)[:, None] * inv
    ac = cols.astype(F32)[:, None] * inv
    cr, sr, cc, sc = jnp.cos(ar), jnp.sin(ar), jnp.cos(ac), jnp.sin(ac)
    z = jnp.zeros_like(sr)
    c = jnp.concatenate([cr, cr, cc, cc] * 2, axis=-1)
    s1 = jnp.concatenate([-sr, z, -sc, z] * 2, axis=-1)
    s2 = jnp.concatenate([z, sr, z, sc] * 2, axis=-1)
    return jnp.concatenate([c, s1, s2], axis=-1)


def kernel(x, c, ctx, c_ctx, w_mod, b_mod, norm1_g, norm2_g, w_in, lam_q1, lam_k1, lam_q2, lam_k2, attn_subln_g, w_attn_o, conf_dw_w, conf_dw_b, conf_ln_g, conf_ln_b, w_conf_o, hy_sc_w, hy_sc_b, hy_w1, hy_b1, hy_w2, hy_b2, hy_freq, hy_w3, hy_decay, hy_bias, w_hy_o, w_out, w_router_group, w_router_expert, w_exp_gate, w_exp_up, w_exp_down, norm_f_g):
    b, s, d = x.shape
    lc = ctx.shape[1]
    depth = w_mod.shape[0]
    t, tc_rows = b * s, b * lc
    cw, hw = conf_dw_w.shape[2], w_hy_o.shape[1]
    dims = dict(B=b, S=s, Lc=lc, T=t, QK=ATT_HEADS * 2 * ATT_DQK, AW=ATT_HEADS * ATT_DV, CW=cw, HW=hw)
    assert b + 1 <= MOD_ROWS and 2 * ATT_DQK == LANES and ATT_DV == LANES and hw == cw
    tm = _largest_tile((1024, 512, 256, 128), s, tc_rows)
    ts = _largest_tile((256, 128), s, lc)
    tn = _largest_tile((512, 256, 128), dims["QK"], dims["AW"], 2 * cw, 3 * hw, d)
    tcomb = _largest_tile((256, 128), s, tc_rows)

    x_all = jnp.concatenate([x.reshape(t, d), ctx.reshape(tc_rows, d)], axis=0)
    cond = jnp.zeros((MOD_ROWS, d), F32).at[:b].set(c).at[b].set(c_ctx)
    mod3 = _modulation(cond, w_mod, b_mod).reshape(depth * MOD_ROWS, 1, 6 * d)
    rope_tab = _rope_table(s)
    tables_lat, emb_lat = _dft_tables(s), _hyena_embedding(s)
    tables_ctx, emb_ctx = _dft_tables(lc), _hyena_embedding(lc)
    hy_params = dict(hy_w1=hy_w1, hy_b1=hy_b1, hy_w2=hy_w2, hy_b2=hy_b2, hy_freq=hy_freq, hy_w3=hy_w3,
                     hy_decay=hy_decay, hy_bias=hy_bias)

    out = None
    for l in range(depth):
        need_ctx = l < depth - 1
        n_rows = t + tc_rows if need_ctx else t
        lam_init = 0.8 - 0.6 * math.exp(-0.3 * l)
        lam = (jnp.exp(jnp.sum(lam_q1[l] * lam_k1[l])) - jnp.exp(jnp.sum(lam_q2[l] * lam_k2[l])) + lam_init)
        lam = lam.reshape(1).astype(F32)

        hl = _adanorm(x_all, norm1_g, mod3, l, dims, tm)
        qkv, conv_in, gates = _in_proj(hl, w_in, l, rope_tab, dims, tm, tn)
        att, att_ctx = _attention(qkv, lam, attn_subln_g, l, lam_init, dims, need_ctx)
        conf = _conformer(conv_in, conf_dw_w, conf_dw_b, conf_ln_g, conf_ln_b, l, dims, n_rows, ts)
        u = _hyena_short(conv_in, hy_sc_w, hy_sc_b, l, dims, n_rows, ts)
        hy = _hyena_long(u, 0, b, s, l, hy_params, tables_lat, emb_lat, dims)
        hy_ctx = _hyena_long(u, t, b, lc, l, hy_params, tables_ctx, emb_ctx, dims) if need_ctx else hy
        merged = _merge(att, att_ctx, conf, hy, hy_ctx, gates, w_attn_o, w_conf_o, w_hy_o, l, n_rows, t, d, tm, tn)
        x_all = _out_proj(merged, w_out, x_all, mod3, l, n_rows, dims, tm, tn)

        w_router = jnp.concatenate(
            [w_router_group[l], jnp.transpose(w_router_expert[l], (1, 0, 2)).reshape(d, N_EXPERTS)], axis=1)
        w_router = jnp.pad(w_router, ((0, 0), (0, LANES - w_router.shape[1])))
        h2, route = _router(x_all, norm2_g, mod3, w_router, l, n_rows, dims, tm)
        buf_tok, buf_w, slot, blk_e, n_used, n_blocks = _dispatch(route, n_rows, MOE_BLOCK)
        xs = _gather_rows(h2, buf_tok, n_used, n_blocks, MOE_BLOCK)
        yb = _experts(xs, buf_w, blk_e, n_used, w_exp_gate, w_exp_up, w_exp_down, l, n_blocks, MOE_BLOCK)
        x_new = _combine(yb, slot, x_all, mod3, l, n_rows, dims, tcomb, None if need_ctx else norm_f_g)
        if need_ctx:
            x_all = x_new
        else:
            out = x_new
    return out.reshape(b, s, d)
```

```python
import functools
import math

import jax
import jax.numpy as jnp
from jax import lax
from jax.experimental import pallas as pl
from jax.experimental.pallas import tpu as pltpu

F32 = jnp.float32
BF16 = jnp.bfloat16

GRID_W = 64
EPS = 1e-6
ATT_HEADS = 8
ATT_DQK = 64
ATT_DV = 2 * ATT_DQK
ROPE_BASE = 10000.0
CONF_K = 31
HY_ORDER = 2
HY_SHORT_K = 3
HY_EMB = 33
HY_BANDS = (HY_EMB - 1) // 2
N_BRANCH = 3
N_GROUPS = 8
E_PER_GROUP = 8
N_EXPERTS = N_GROUPS * E_PER_GROUP

LANES = 128
SUBLANES = 8
VMEM_LIMIT_BYTES = 56 * 1024 * 1024

CONF_HALO = 16
HY_HALO = 8
MOE_BLOCK = 256
ATT_UNROLL = 8
MOD_ROWS = 8


def _params(*sem):
    return pltpu.CompilerParams(dimension_semantics=sem, vmem_limit_bytes=VMEM_LIMIT_BYTES)


def _largest_tile(cands, *dims):
    for c in cands:
        if all(d % c == 0 for d in dims):
            return c
    raise ValueError(f"no tile in {cands} divides {dims}")


def _dot(a, b):
    return jnp.dot(a, b, preferred_element_type=F32)


def _split_bf16(x):
    hi = x.astype(BF16)
    lo = (x - hi.astype(F32)).astype(BF16)
    return hi, lo


def _dot3(a, b):
    ah, al = _split_bf16(a)
    bh, bl = _split_bf16(b)
    return _dot(ah, bh) + _dot(al, bh) + _dot(ah, bl)


def _mod_kernel(c_ref, w_ref, b_ref, o_ref):
    c = c_ref[...]
    s = c * jax.nn.sigmoid(c)
    o_ref[...] = _dot3(s, w_ref[...]) + b_ref[...]


def _modulation(cond, w_mod, b_mod):
    depth, d, n = w_mod.shape
    tn = _largest_tile((512, 256, 128), n)
    return pl.pallas_call(
        _mod_kernel,
        out_shape=jax.ShapeDtypeStruct((depth, MOD_ROWS, n), F32),
        grid=(depth, n // tn),
        in_specs=[
            pl.BlockSpec((MOD_ROWS, d), lambda l, j: (0, 0)),
            pl.BlockSpec((None, d, tn), lambda l, j: (l, 0, j)),
            pl.BlockSpec((None, 1, tn), lambda l, j: (l, 0, j)),
        ],
        out_specs=pl.BlockSpec((None, MOD_ROWS, tn), lambda l, j: (l, 0, j)),
        compiler_params=_params("parallel", "parallel"),
        name="modulation",
    )(cond, w_mod, b_mod.reshape(depth, 1, n))


def _ada_norm_value(x, g, shift, scale):
    ms = jnp.mean(x * x, axis=-1, keepdims=True)
    y = x * lax.rsqrt(ms + EPS) * g
    return y * (1.0 + scale) + shift


def _adanorm_kernel(x_ref, g_ref, sh_ref, sc_ref, o_ref):
    o_ref[...] = _ada_norm_value(x_ref[...], g_ref[...], sh_ref[...], sc_ref[...]).astype(o_ref.dtype)


def _mod_row_map(layer, tm, n_lat, s, b, col):
    def index_map(i, *_):
        row = jnp.where(i < n_lat, (i * tm) // s, b)
        return (layer * MOD_ROWS + row, 0, col)
    return index_map


def _adanorm(x_all, norm_g, mod3, layer, dims, tm):
    r, d = x_all.shape
    n_lat = dims["T"] // tm
    mm = functools.partial(_mod_row_map, layer, tm, n_lat, dims["S"], dims["B"])
    return pl.pallas_call(
        _adanorm_kernel,
        out_shape=jax.ShapeDtypeStruct((r, d), BF16),
        grid=(r // tm,),
        in_specs=[
            pl.BlockSpec((tm, d), lambda i: (i, 0)),
            pl.BlockSpec((None, 1, d), lambda i: (layer, 0, 0)),
            pl.BlockSpec((None, 1, d), mm(0)),
            pl.BlockSpec((None, 1, d), mm(1)),
        ],
        out_specs=pl.BlockSpec((tm, d), lambda i: (i, 0)),
        compiler_params=_params("parallel"),
        name="adanorm1",
    )(x_all, norm_g.reshape(norm_g.shape[0], 1, d), mod3, mod3)


def _qkv_kernel(a_ref, w_ref, tab_ref, o_ref, *, n_lat, tn, qk_cols):
    i = pl.program_id(0)
    j = pl.program_id(1)
    acc = _dot(a_ref[...], w_ref[...].astype(BF16))
    col0 = j * tn
    scale = jnp.where(col0 < qk_cols, ATT_DQK ** -0.5 * math.log2(math.e), 1.0).astype(F32)
    do_rope = jnp.logical_and(i < n_lat, col0 < 2 * qk_cols)

    @pl.when(do_rope)
    def _():
        c = tab_ref[:, 0:LANES]
        s1 = tab_ref[:, LANES:2 * LANES]
        s2 = tab_ref[:, 2 * LANES:3 * LANES]
        for h in range(tn // LANES):
            xh = acc[:, h * LANES:(h + 1) * LANES] * scale
            rot = xh * c + pltpu.roll(xh, LANES - 16, 1) * s1 + pltpu.roll(xh, 16, 1) * s2
            o_ref[:, h * LANES:(h + 1) * LANES] = rot.astype(o_ref.dtype)

    @pl.when(jnp.logical_not(do_rope))
    def _():
        o_ref[...] = (acc * scale).astype(o_ref.dtype)


def _plain_mm_kernel(a_ref, w_ref, o_ref):
    o_ref[...] = _dot(a_ref[...], w_ref[...].astype(BF16)).astype(o_ref.dtype)


def _sigmoid_mm_kernel(a_ref, w_ref, o_ref):
    o_ref[...] = jax.nn.sigmoid(_dot(a_ref[...], w_ref[...].astype(BF16))).astype(o_ref.dtype)


def _in_proj(hl, w_in, layer, rope_tab, dims, tm, tn):
    r, d = hl.shape
    qk, aw, cw, hw = dims["QK"], dims["AW"], dims["CW"], dims["HW"]
    n_lat = dims["T"] // tm
    s_tiles = dims["S"] // tm
    a_spec = pl.BlockSpec((tm, d), lambda i, j: (i, 0))

    def w_spec(col_off):
        off = col_off // tn
        return pl.BlockSpec((None, d, tn), lambda i, j: (layer, 0, j + off))

    def call(kernel, col_off, n_cols, dtype, name, extra_in=(), extra_specs=()):
        return pl.pallas_call(
            kernel,
            out_shape=jax.ShapeDtypeStruct((r, n_cols), dtype),
            grid=(r // tm, n_cols // tn),
            in_specs=[a_spec, w_spec(col_off), *extra_specs],
            out_specs=pl.BlockSpec((tm, tn), lambda i, j: (i, j)),
            compiler_params=_params("parallel", "arbitrary"),
            name=name,
        )(hl, w_in, *extra_in)

    qkv = call(
        functools.partial(_qkv_kernel, n_lat=n_lat, tn=tn, qk_cols=qk), 0, 2 * qk + aw, BF16, "in_proj_qkv",
        extra_in=(rope_tab,),
        extra_specs=(pl.BlockSpec((tm, 3 * LANES), lambda i, j: (i % s_tiles, 0)),))
    conv_in = call(_plain_mm_kernel, 2 * qk + aw, 2 * cw + 3 * hw, F32, "in_proj_conv")
    gates = call(_sigmoid_mm_kernel, 2 * qk + aw + 2 * cw + 3 * hw, N_BRANCH * d, BF16, "in_proj_gates")
    return qkv, conv_in, gates


def _attn_kernel(lam_ref, q_ref, g_ref, kc_ref, vc_ref, *rest, n_lat_chunks, tk, out_scale, unroll):
    if n_lat_chunks:
        kl_ref, vl_ref, o_ref, vce, vle = rest
    else:
        o_ref, vce = rest

    @pl.when(pl.program_id(2) == 0)
    def _():
        vce[:, 0:ATT_DV] = vc_ref[...]
        vce[:, ATT_DV:] = jnp.ones((vce.shape[0], ATT_DV), BF16)
        if n_lat_chunks:
            vle[:, 0:ATT_DV] = vl_ref[...]
            vle[:, ATT_DV:] = jnp.ones((vle.shape[0], ATT_DV), BF16)

    q = q_ref[...]
    lane = lax.broadcasted_iota(jnp.int32, q.shape, 1)
    zero = jnp.zeros_like(q)
    q1 = jnp.where(lane < ATT_DQK, q, zero)
    q2 = jnp.where(lane >= ATT_DQK, q, zero)
    tq = q.shape[0]
    nt = (((1,), (1,)), ((), ()))

    def one_map(qm, k, v, m, acc):
        s = lax.dot_general(qm, k, nt, preferred_element_type=F32)
        m_new = jnp.maximum(m, jnp.max(s, axis=-1, keepdims=True))
        alpha = jnp.exp2(m - m_new)
        p = jnp.exp2((s - m_new).astype(BF16))
        return m_new, alpha * acc + _dot(p, v)

    def step(k, v, carry):
        m1, a1, m2, a2 = carry
        m1, a1 = one_map(q1, k, v, m1, a1)
        m2, a2 = one_map(q2, k, v, m2, a2)
        return m1, a1, m2, a2

    neg = jnp.full((tq, 1), -jnp.inf, F32)
    za = jnp.zeros((tq, 2 * ATT_DV), F32)
    carry = step(kc_ref[...], vce[...], (neg, za, neg, za))
    if n_lat_chunks:
        def body(c, carry):
            off = pl.multiple_of(c * tk, tk)
            return step(kl_ref[pl.ds(off, tk), :], vle[pl.ds(off, tk), :], carry)
        carry = lax.fori_loop(0, n_lat_chunks, body, carry, unroll=unroll)
    _, a1, _, a2 = carry
    o = a1[:, 0:ATT_DV] / a1[:, ATT_DV:] - lam_ref[0] * (a2[:, 0:ATT_DV] / a2[:, ATT_DV:])
    ms = jnp.mean(o * o, axis=-1, keepdims=True)
    o = o * lax.rsqrt(ms + 1e-5) * g_ref[...] * out_scale
    o_ref[...] = o.astype(o_ref.dtype)


def _attention(qkv, lam, subln_g, layer, lam_init, dims, with_ctx):
    b, s, lc, t = dims["B"], dims["S"], dims["Lc"], dims["T"]
    h = ATT_HEADS
    tq = _largest_tile((512, 256, 128), s)
    tk = _largest_tile((512, 256, 128), s)
    tqc = _largest_tile((512, 256, 128), lc)
    smem = pl.BlockSpec(memory_space=pltpu.SMEM)
    g3 = subln_g.reshape(subln_g.shape[0], 1, ATT_DV)
    g_spec = pl.BlockSpec((None, 1, ATT_DV), lambda bi, hi, ti: (layer, 0, 0))
    kc_spec = pl.BlockSpec((lc, LANES), lambda bi, hi, ti: (t // lc + bi, h + hi))
    vc_spec = pl.BlockSpec((lc, LANES), lambda bi, hi, ti: (t // lc + bi, 2 * h + hi))
    out_scale = 1.0 - lam_init

    sem = ("arbitrary", "arbitrary", "arbitrary")
    att = pl.pallas_call(
        functools.partial(_attn_kernel, n_lat_chunks=s // tk, tk=tk, out_scale=out_scale, unroll=ATT_UNROLL),
        out_shape=jax.ShapeDtypeStruct((t, h * ATT_DV), BF16),
        grid=(b, h, s // tq),
        in_specs=[
            smem,
            pl.BlockSpec((tq, LANES), lambda bi, hi, ti: (bi * (s // tq) + ti, hi)),
            g_spec, kc_spec, vc_spec,
            pl.BlockSpec((s, LANES), lambda bi, hi, ti: (bi, h + hi)),
            pl.BlockSpec((s, LANES), lambda bi, hi, ti: (bi, 2 * h + hi)),
        ],
        out_specs=pl.BlockSpec((tq, LANES), lambda bi, hi, ti: (bi * (s // tq) + ti, hi)),
        scratch_shapes=[pltpu.VMEM((lc, 2 * ATT_DV), BF16), pltpu.VMEM((s, 2 * ATT_DV), BF16)],
        compiler_params=_params(*sem),
        name="diff_attn_latent",
    )(lam, qkv, g3, qkv, qkv, qkv, qkv)
    if not with_ctx:
        return att, att

    att_ctx = pl.pallas_call(
        functools.partial(_attn_kernel, n_lat_chunks=0, tk=tk, out_scale=out_scale, unroll=1),
        out_shape=jax.ShapeDtypeStruct((b * lc, h * ATT_DV), BF16),
        grid=(b, h, lc // tqc),
        in_specs=[
            smem,
            pl.BlockSpec((tqc, LANES), lambda bi, hi, ti: ((t + bi * lc) // tqc + ti, hi)),
            g_spec, kc_spec, vc_spec,
        ],
        out_specs=pl.BlockSpec((tqc, LANES), lambda bi, hi, ti: (bi * (lc // tqc) + ti, hi)),
        scratch_shapes=[pltpu.VMEM((lc, 2 * ATT_DV), BF16)],
        compiler_params=_params(*sem),
        name="diff_attn_ctx",
    )(lam, qkv, g3, qkv, qkv)
    return att, att_ctx


def _seq_edges(i, n_lat, tps_lat, tps_ctx):
    is_lat = i < n_lat
    pos = jnp.where(is_lat, i % tps_lat, (i - n_lat) % tps_ctx)
    n = jnp.where(is_lat, tps_lat, tps_ctx)
    return pos == 0, pos == n - 1


def _halo_specs(ts, halo, width, col, n_rows):
    per = ts // halo
    last = n_rows // halo - 1
    prev = pl.BlockSpec((halo, width), lambda i, *_: (jnp.maximum(i * per - 1, 0), col(*_)))
    cur = pl.BlockSpec((ts, width), lambda i, *_: (i, col(*_)))
    nxt = pl.BlockSpec((halo, width), lambda i, *_: (jnp.minimum((i + 1) * per, last), col(*_)))
    return prev, cur, nxt


def _conf_kernel(ap_ref, ac_ref, an_ref, gp_ref, gc_ref, gn_ref, w_ref, b_ref, lg_ref, lb_ref, o_ref, ybuf,
                 *, ts, n_lat, tps_lat, tps_ctx, row_chunk):
    first, last = _seq_edges(pl.program_id(0), n_lat, tps_lat, tps_ctx)
    halo = CONF_HALO

    def glu(a, g):
        return a * jax.nn.sigmoid(g)

    prev = glu(ap_ref[...], gp_ref[...])
    nxt = glu(an_ref[...], gn_ref[...])
    ybuf[0:halo, :] = jnp.where(first, jnp.zeros_like(prev), prev)
    ybuf[halo:halo + ts, :] = glu(ac_ref[...], gc_ref[...])
    ybuf[halo + ts:, :] = jnp.where(last, jnp.zeros_like(nxt), nxt)
    base = halo - CONF_K // 2
    for r0 in range(0, ts, row_chunk):
        acc = jnp.zeros((row_chunk, ybuf.shape[1]), F32) + b_ref[...]
        for j in range(CONF_K):
            acc = acc + w_ref[j:j + 1, :] * ybuf[r0 + base + j:r0 + base + j + row_chunk, :]
        mu = jnp.mean(acc, axis=-1, keepdims=True)
        xc = acc - mu
        var = jnp.mean(xc * xc, axis=-1, keepdims=True)
        y = xc * lax.rsqrt(var + 1e-5) * lg_ref[...] + lb_ref[...]
        o_ref[r0:r0 + row_chunk, :] = (y * jax.nn.sigmoid(y)).astype(o_ref.dtype)


def _conformer(conv_in, dw_w, dw_b, ln_g, ln_b, layer, dims, n_rows, ts):
    cw = dims["CW"]
    depth = dw_w.shape[0]
    n_lat = dims["T"] // ts
    a_specs = _halo_specs(ts, CONF_HALO, cw, lambda: 0, conv_in.shape[0])
    g_specs = _halo_specs(ts, CONF_HALO, cw, lambda: 1, conv_in.shape[0])
    vec = lambda: pl.BlockSpec((None, 1, cw), lambda i: (layer, 0, 0))
    kern = functools.partial(_conf_kernel, ts=ts, n_lat=n_lat, tps_lat=dims["S"] // ts, tps_ctx=dims["Lc"] // ts,
                             row_chunk=min(32, ts))
    return pl.pallas_call(
        kern,
        out_shape=jax.ShapeDtypeStruct((n_rows, cw), BF16),
        grid=(n_rows // ts,),
        in_specs=[*a_specs, *g_specs,
                  pl.BlockSpec((None, CONF_K, cw), lambda i: (layer, 0, 0)), vec(), vec(), vec()],
        out_specs=pl.BlockSpec((ts, cw), lambda i: (i, 0)),
        scratch_shapes=[pltpu.VMEM((ts + 2 * CONF_HALO, cw), F32)],
        compiler_params=_params("parallel"),
        name="conformer_conv",
    )(conv_in, conv_in, conv_in, conv_in, conv_in, conv_in, dw_w,
      dw_b.reshape(depth, 1, cw), ln_g.reshape(depth, 1, cw), ln_b.reshape(depth, 1, cw))


def _hy_short_kernel(p_ref, c_ref, n_ref, w_ref, b_ref, o_ref, ybuf, *, ts, n_lat, tps_lat, tps_ctx):
    first, last = _seq_edges(pl.program_id(0), n_lat, tps_lat, tps_ctx)
    halo = HY_HALO
    prev = p_ref[...]
    nxt = n_ref[...]
    ybuf[0:halo, :] = jnp.where(first, jnp.zeros_like(prev), prev)
    ybuf[halo:halo + ts, :] = c_ref[...]
    ybuf[halo + ts:, :] = jnp.where(last, jnp.zeros_like(nxt), nxt)
    acc = b_ref[...] + w_ref[0:1, :] * ybuf[halo - 1:halo - 1 + ts, :]
    acc = acc + w_ref[1:2, :] * ybuf[halo:halo + ts, :]
    acc = acc + w_ref[2:3, :] * ybuf[halo + 1:halo + 1 + ts, :]
    o_ref[...] = acc


def _hyena_short(conv_in, sc_w, sc_b, layer, dims, n_rows, ts):
    cw, hw = dims["CW"], dims["HW"]
    depth = sc_w.shape[0]
    n_lat = dims["T"] // ts
    col0 = 2 * cw // hw
    specs = _halo_specs(ts, HY_HALO, hw, lambda m: col0 + m, conv_in.shape[0])
    kern = functools.partial(_hy_short_kernel, ts=ts, n_lat=n_lat, tps_lat=dims["S"] // ts,
                             tps_ctx=dims["Lc"] // ts)
    return pl.pallas_call(
        kern,
        out_shape=jax.ShapeDtypeStruct((n_rows, 3 * hw), F32),
        grid=(n_rows // ts, 3),
        in_specs=[*specs,
                  pl.BlockSpec((None, HY_SHORT_K, hw), lambda i, m: (layer, 0, m)),
                  pl.BlockSpec((None, 1, hw), lambda i, m: (layer, 0, m))],
        out_specs=pl.BlockSpec((ts, hw), lambda i, m: (i, m)),
        scratch_shapes=[pltpu.VMEM((ts + 2 * HY_HALO, hw), F32)],
        compiler_params=_params("parallel", "arbitrary"),
        name="hyena_short_conv",
    )(conv_in, conv_in, conv_in, sc_w, sc_b.reshape(depth, 1, 3 * hw))


def _filter_kernel(emb_ref, w1_ref, b1_ref, f_ref, w2_ref, b2_ref, w3_ref, dec_ref, h_ref, asum_ref, *, hw, tl):
    i = pl.program_id(0)
    hp = lax.Precision.HIGHEST
    emb = emb_ref[...]
    t = emb[:, 0:1]
    h1 = jnp.sin(f_ref[0:1, :] * (jnp.dot(emb, w1_ref[...], precision=hp, preferred_element_type=F32) + b1_ref[...]))
    h2 = jnp.sin(f_ref[1:2, :] * (jnp.dot(h1, w2_ref[...], precision=hp, preferred_element_type=F32) + b2_ref[...]))
    h = jnp.dot(h2, w3_ref[...], precision=hp, preferred_element_type=F32)
    h = h * jnp.exp(-t * jnp.abs(dec_ref[...]))
    col = lax.broadcasted_iota(jnp.int32, h.shape, 1)
    row = lax.broadcasted_iota(jnp.int32, h.shape, 0) + i * tl
    backward = (col // hw) % 2 == 1
    h = jnp.where(jnp.logical_and(backward, row == 0), 0.0, h)
    h_ref[...] = h.astype(h_ref.dtype)

    @pl.when(i == 0)
    def _():
        asum_ref[...] = jnp.zeros_like(asum_ref)

    part = jnp.abs(h).reshape(tl // SUBLANES, SUBLANES, h.shape[1]).sum(axis=0)
    asum_ref[...] += part


def _hyena_filter(emb, w1, b1, freq, w2, b2, w3, decay, layer, hw):
    l, ke = emb.shape
    depth, _, ffn = w1.shape
    nf = w3.shape[2]
    tl = _largest_tile((512, 256, 128), l)
    w1p = jnp.pad(w1, ((0, 0), (0, ke - w1.shape[1]), (0, 0)))
    vec = lambda n: pl.BlockSpec((None, 1, n), lambda i: (layer, 0, 0))
    return pl.pallas_call(
        functools.partial(_filter_kernel, hw=hw, tl=tl),
        out_shape=(jax.ShapeDtypeStruct((l, nf), BF16), jax.ShapeDtypeStruct((SUBLANES, nf), F32)),
        grid=(l // tl,),
        in_specs=[
            pl.BlockSpec((tl, ke), lambda i: (i, 0)),
            pl.BlockSpec((None, ke, ffn), lambda i: (layer, 0, 0)),
            vec(ffn),
            pl.BlockSpec((None, 2, ffn), lambda i: (layer, 0, 0)),
            pl.BlockSpec((None, ffn, ffn), lambda i: (layer, 0, 0)),
            vec(ffn),
            pl.BlockSpec((None, ffn, nf), lambda i: (layer, 0, 0)),
            vec(nf),
        ],
        out_specs=(pl.BlockSpec((tl, nf), lambda i: (i, 0)), pl.BlockSpec((SUBLANES, nf), lambda i: (0, 0))),
        compiler_params=_params("arbitrary"),
        name="hyena_filter_mlp",
    )(emb, w1p, b1.reshape(depth, 1, ffn), freq, w2, b2.reshape(depth, 1, ffn), w3, decay.reshape(depth, 1, nf))


def _bin_scale(i, tm, n_bins):
    row = lax.broadcasted_iota(jnp.int32, (tm, 1), 0) + i * tm
    is0 = row == 0
    return is0, jnp.where(is0, 1.0 / (2 * n_bins), 2.0 / (2 * n_bins)).astype(F32)


def _filter_dft_kernel(c_ref, s_ref, hf_ref, hb_ref, af_ref, ab_ref, kre_ref, kim_ref, *, tm, n_bins):
    i = pl.program_id(1)
    c = c_ref[...]
    s = s_ref[...]
    fre, fim = _dot(c, hf_ref[...]), _dot(s, hf_ref[...])
    bre, bim = _dot(c, hb_ref[...]), _dot(s, hb_ref[...])
    norm = jnp.sum(af_ref[...], axis=0, keepdims=True) + jnp.sum(ab_ref[...], axis=0, keepdims=True)
    is0, cs = _bin_scale(i, tm, n_bins)
    w = cs / norm
    kre_ref[...] = (fre + bre) * w
    kim_ref[...] = jnp.where(is0, fim + bim, fim - bim) * w


def _filter_spectrum(h, asum, c_tab, s_tab, hw):
    l, nf = h.shape
    n_ord = nf // (2 * hw)
    tm = _largest_tile((512, 256, 128), l)
    out = jax.ShapeDtypeStruct((l, n_ord * hw), F32)
    tab = pl.BlockSpec((tm, l), lambda o, i: (i, 0))
    return pl.pallas_call(
        functools.partial(_filter_dft_kernel, tm=tm, n_bins=l),
        out_shape=(out, out),
        grid=(n_ord, l // tm),
        in_specs=[tab, tab,
                  pl.BlockSpec((l, hw), lambda o, i: (0, 2 * o)),
                  pl.BlockSpec((l, hw), lambda o, i: (0, 2 * o + 1)),
                  pl.BlockSpec((SUBLANES, hw), lambda o, i: (0, 2 * o)),
                  pl.BlockSpec((SUBLANES, hw), lambda o, i: (0, 2 * o + 1))],
        out_specs=(pl.BlockSpec((tm, hw), lambda o, i: (i, o)), pl.BlockSpec((tm, hw), lambda o, i: (i, o))),
        compiler_params=_params("parallel", "arbitrary"),
        name="hyena_filter_dft",
    )(c_tab, s_tab, h, h, asum, asum)


def _fwd_dft_kernel(c_ref, s_ref, z_ref, kre_ref, kim_ref, yre_ref, yim_ref, zb, *, tm):
    i = pl.program_id(1)

    @pl.when(i == 0)
    def _():
        zb[...] = z_ref[...].astype(BF16)

    xre = _dot(c_ref[...], zb[...])
    xim = _dot(s_ref[...], zb[...])
    kre, kim = kre_ref[...], kim_ref[...]
    is0 = (lax.broadcasted_iota(jnp.int32, (tm, 1), 0) + i * tm) == 0
    yre = xre * kre - jnp.where(is0, 0.0, xim * kim)
    yim = jnp.where(is0, xim * kim, xre * kim + xim * kre)
    yre_ref[...] = yre.astype(BF16)
    yim_ref[...] = yim.astype(BF16)


def _fwd_dft(z_arr, z_row0, z_col, kre, kim, order, c_tab, s_tab, b, l, hw):
    tm = _largest_tile((512, 256, 128), l)
    out = jax.ShapeDtypeStruct((b * l, hw), BF16)
    tab = pl.BlockSpec((tm, l), lambda bi, i: (i, 0))
    kspec = pl.BlockSpec((tm, hw), lambda bi, i: (i, order))
    ospec = pl.BlockSpec((tm, hw), lambda bi, i: (bi * (l // tm) + i, 0))
    return pl.pallas_call(
        functools.partial(_fwd_dft_kernel, tm=tm),
        out_shape=(out, out),
        grid=(b, l // tm),
        in_specs=[tab, tab, pl.BlockSpec((l, hw), lambda bi, i: (z_row0 // l + bi, z_col)), kspec, kspec],
        out_specs=(ospec, ospec),
        scratch_shapes=[pltpu.VMEM((l, hw), BF16)],
        compiler_params=_params("parallel", "arbitrary"),
        name="hyena_fwd_dft",
    )(c_tab, s_tab, z_arr, kre, kim)


def _inv_dft_kernel(c_ref, st_ref, yre_ref, yim_ref, gate_ref, z_ref, bias_ref, o_ref):
    y = _dot(c_ref[...], yre_ref[...]) + _dot(st_ref[...], yim_ref[...])
    o_ref[...] = (gate_ref[...] * (y + bias_ref[...] * z_ref[...])).astype(o_ref.dtype)


def _inv_dft(yre, yim, u, u_row0, gate_col, z_arr, z_row0, z_col, bias, layer, order, c_tab, st_tab, b, l, hw,
             out_dtype):
    tm = _largest_tile((512, 256, 128), l)
    per = l // tm
    tab = pl.BlockSpec((tm, l), lambda bi, i: (i, 0))
    yspec = pl.BlockSpec((l, hw), lambda bi, i: (bi, 0))
    return pl.pallas_call(
        _inv_dft_kernel,
        out_shape=jax.ShapeDtypeStruct((b * l, hw), out_dtype),
        grid=(b, per),
        in_specs=[tab, tab, yspec, yspec,
                  pl.BlockSpec((tm, hw), lambda bi, i: (u_row0 // tm + bi * per + i, gate_col)),
                  pl.BlockSpec((tm, hw), lambda bi, i: (z_row0 // tm + bi * per + i, z_col)),
                  pl.BlockSpec((None, None, 1, hw), lambda bi, i: (layer, order, 0, 0))],
        out_specs=pl.BlockSpec((tm, hw), lambda bi, i: (bi * per + i, 0)),
        compiler_params=_params("parallel", "arbitrary"),
        name="hyena_inv_dft",
    )(c_tab, st_tab, yre, yim, u, z_arr, bias.reshape(bias.shape[0], bias.shape[1], 1, hw))


def _dft_tables(l):
    n = 2 * l
    k = jnp.arange(l, dtype=jnp.int32)
    m = (k[:, None] * k[None, :]) % n
    ang = m.astype(F32) * (2.0 * math.pi / n)
    c = jnp.cos(ang)
    s = -jnp.sin(ang)
    alt = jnp.where(k % 2 == 0, 1.0, -1.0).astype(F32)
    s = jnp.where(k[:, None] == 0, alt[None, :], s)
    return c.astype(BF16), s.astype(BF16), s.T.astype(BF16)


def _hyena_embedding(l):
    n = jnp.arange(l, dtype=F32)
    t = n / max(l - 1, 1)
    w = 2.0 * math.pi * n / l
    f = jnp.linspace(1e-4, HY_BANDS - 1, HY_BANDS, dtype=F32)
    fw = w[:, None] * f[None, :]
    emb = jnp.concatenate([t[:, None], jnp.cos(fw), -jnp.sin(fw)], axis=-1)
    return jnp.pad(emb, ((0, 0), (0, LANES - HY_EMB)))


def _hyena_long(u, u_row0, b, l, layer, p, tables, emb, dims):
    hw = dims["HW"]
    c_tab, s_tab, st_tab = tables
    h, asum = _hyena_filter(emb, p["hy_w1"], p["hy_b1"], p["hy_freq"], p["hy_w2"], p["hy_b2"], p["hy_w3"],
                            p["hy_decay"], layer, hw)
    kre, kim = _filter_spectrum(h, asum, c_tab, s_tab, hw)
    yre, yim = _fwd_dft(u, u_row0, 2, kre, kim, 0, c_tab, s_tab, b, l, hw)
    z1 = _inv_dft(yre, yim, u, u_row0, 0, u, u_row0, 2, p["hy_bias"], layer, 0, c_tab, st_tab, b, l, hw, F32)
    yre, yim = _fwd_dft(z1, 0, 0, kre, kim, 1, c_tab, s_tab, b, l, hw)
    return _inv_dft(yre, yim, u, u_row0, 1, z1, 0, 0, p["hy_bias"], layer, 1, c_tab, st_tab, b, l, hw, BF16)


def _merge_kernel(attl_ref, attc_ref, conf_ref, hyl_ref, hyc_ref, wa_ref, wc_ref, wh_ref, g0_ref, g1_ref, g2_ref,
                  o_ref, *, n_lat):
    is_lat = pl.program_id(0) < n_lat
    a = _dot(jnp.where(is_lat, attl_ref[...], attc_ref[...]), wa_ref[...].astype(BF16))
    c = _dot(conf_ref[...], wc_ref[...].astype(BF16))
    h = _dot(jnp.where(is_lat, hyl_ref[...], hyc_ref[...]), wh_ref[...].astype(BF16))
    m = g0_ref[...].astype(F32) * a + g1_ref[...].astype(F32) * c + g2_ref[...].astype(F32) * h
    o_ref[...] = m.astype(o_ref.dtype)


def _merge(att, att_ctx, conf, hy, hy_ctx, gates, w_attn_o, w_conf_o, w_hy_o, layer, n_rows, n_lat_rows, d, tm, tn):
    aw, cw, hw = att.shape[1], conf.shape[1], hy.shape[1]
    gb = d // tn
    n_lat = n_lat_rows // tm
    lat_map = lambda i, j: (jnp.minimum(i, n_lat - 1), 0)
    ctx_map = lambda i, j: (jnp.maximum(i - n_lat, 0), 0)
    return pl.pallas_call(
        functools.partial(_merge_kernel, n_lat=n_lat),
        out_shape=jax.ShapeDtypeStruct((n_rows, d), BF16),
        grid=(n_rows // tm, d // tn),
        in_specs=[
            pl.BlockSpec((tm, aw), lat_map),
            pl.BlockSpec((tm, aw), ctx_map),
            pl.BlockSpec((tm, cw), lambda i, j: (i, 0)),
            pl.BlockSpec((tm, hw), lat_map),
            pl.BlockSpec((tm, hw), ctx_map),
            pl.BlockSpec((None, aw, tn), lambda i, j: (layer, 0, j)),
            pl.BlockSpec((None, cw, tn), lambda i, j: (layer, 0, j)),
            pl.BlockSpec((None, hw, tn), lambda i, j: (layer, 0, j)),
            pl.BlockSpec((tm, tn), lambda i, j: (i, j)),
            pl.BlockSpec((tm, tn), lambda i, j: (i, gb + j)),
            pl.BlockSpec((tm, tn), lambda i, j: (i, 2 * gb + j)),
        ],
        out_specs=pl.BlockSpec((tm, tn), lambda i, j: (i, j)),
        compiler_params=_params("parallel", "arbitrary"),
        name="branch_proj_merge",
    )(att, att_ctx, conf, hy, hy_ctx, w_attn_o, w_conf_o, w_hy_o, gates, gates, gates)


def _out_proj_kernel(a_ref, w_ref, x_ref, g_ref, o_ref):
    o_ref[...] = x_ref[...] + g_ref[...] * _dot(a_ref[...], w_ref[...].astype(BF16))


def _out_proj(merged, w_out, x_all, mod3, layer, n_rows, dims, tm, tn):
    d = x_all.shape[1]
    n_lat = dims["T"] // tm
    mm = functools.partial(_mod_row_map, layer, tm, n_lat, dims["S"], dims["B"])

    def gate_map(i, j):
        row, z, _ = mm(0)(i)
        return (row, z, 2 * (d // tn) + j)

    return pl.pallas_call(
        _out_proj_kernel,
        out_shape=jax.ShapeDtypeStruct((n_rows, d), F32),
        grid=(n_rows // tm, d // tn),
        in_specs=[
            pl.BlockSpec((tm, d), lambda i, j: (i, 0)),
            pl.BlockSpec((None, d, tn), lambda i, j: (layer, 0, j)),
            pl.BlockSpec((tm, tn), lambda i, j: (i, j)),
            pl.BlockSpec((None, 1, tn), gate_map),
        ],
        out_specs=pl.BlockSpec((tm, tn), lambda i, j: (i, j)),
        compiler_params=_params("parallel", "arbitrary"),
        name="out_proj_residual",
    )(merged, w_out, x_all, mod3)


def _router_kernel(x_ref, g_ref, sh_ref, sc_ref, wr_ref, h_ref, r_ref):
    h = _ada_norm_value(x_ref[...], g_ref[...], sh_ref[...], sc_ref[...])
    h_ref[...] = h
    logits = _dot3(h, wr_ref[...])
    lane = lax.broadcasted_iota(jnp.int32, logits.shape, 1).astype(F32)
    ninf = jnp.float32(-jnp.inf)
    big = jnp.float32(LANES)

    def first_argmax(v):
        mx = jnp.max(v, axis=-1, keepdims=True)
        idx = jnp.min(jnp.where(v == mx, lane, big), axis=-1, keepdims=True)
        return mx, idx

    lg = jnp.where(lane < N_GROUPS, logits, ninf)
    gmax, gidx = first_argmax(lg)
    p_grp = 1.0 / jnp.sum(jnp.exp(lg - gmax), axis=-1, keepdims=True)
    lo = N_GROUPS + gidx * E_PER_GROUP
    le = jnp.where(jnp.logical_and(lane >= lo, lane < lo + E_PER_GROUP), logits, ninf)
    e1, i1 = first_argmax(le)
    e2, i2 = first_argmax(jnp.where(lane == i1, ninf, le))
    t = jnp.exp(e2 - e1)
    w1 = p_grp / (1.0 + t)
    w2 = p_grp * t / (1.0 + t)
    out = jnp.where(lane == 0, i1 - N_GROUPS,
                    jnp.where(lane == 1, i2 - N_GROUPS,
                              jnp.where(lane == 2, w1, jnp.where(lane == 3, w2, 0.0))))
    r_ref[...] = out


def _router(x_all, norm_g, mod3, w_router, layer, n_rows, dims, tm):
    r, d = x_all.shape
    n_lat = dims["T"] // tm
    mm = functools.partial(_mod_row_map, layer, tm, n_lat, dims["S"], dims["B"])
    return pl.pallas_call(
        _router_kernel,
        out_shape=(jax.ShapeDtypeStruct((n_rows, d), F32), jax.ShapeDtypeStruct((n_rows, LANES), F32)),
        grid=(n_rows // tm,),
        in_specs=[
            pl.BlockSpec((tm, d), lambda i: (i, 0)),
            pl.BlockSpec((None, 1, d), lambda i: (layer, 0, 0)),
            pl.BlockSpec((None, 1, d), mm(3)),
            pl.BlockSpec((None, 1, d), mm(4)),
            pl.BlockSpec((d, LANES), lambda i: (0, 0)),
        ],
        out_specs=(pl.BlockSpec((tm, d), lambda i: (i, 0)), pl.BlockSpec((tm, LANES), lambda i: (i, 0))),
        compiler_params=_params("parallel"),
        name="adanorm2_router",
    )(x_all, norm_g.reshape(norm_g.shape[0], 1, d), mod3, mod3, w_router)


def _dispatch(route, n_tok, bm):
    eid = route[:, 0:2].astype(jnp.int32).reshape(-1)
    wts = route[:, 2:4].reshape(-1)
    a = 2 * n_tok
    iota = jnp.arange(a, dtype=jnp.int32)
    e_s, order = lax.sort_key_val(eid, iota)
    counts = jnp.sum((eid[:, None] == jnp.arange(N_EXPERTS, dtype=jnp.int32)[None, :]).astype(jnp.int32), axis=0)
    pcounts = (counts + bm - 1) // bm * bm
    pend = jnp.cumsum(pcounts)
    pstart = pend - pcounts
    cstart = jnp.cumsum(counts) - counts
    dest = pstart[e_s] + (iota - cstart[e_s])
    n_blocks = -(-a // bm) + N_EXPERTS
    p = n_blocks * bm
    blk_e = jnp.minimum(jnp.searchsorted(pend, jnp.arange(n_blocks, dtype=jnp.int32) * bm, side='right'),
                        N_EXPERTS - 1).astype(jnp.int32)
    pidx = jnp.arange(p, dtype=jnp.int32)
    e_slot = blk_e[pidx // bm]
    off = pidx - pstart[e_slot]
    valid = jnp.logical_and(off >= 0, off < counts[e_slot])
    src = order[jnp.clip(cstart[e_slot] + off, 0, a - 1)]
    buf_tok = jnp.where(valid, src // 2, 0).astype(jnp.int32)
    buf_w = jnp.where(valid, wts[src], 0.0)
    _, slot = lax.sort_key_val(order, dest)
    n_used = (pend[-1] // bm).astype(jnp.int32).reshape(1)
    return buf_tok, buf_w, slot, blk_e, n_used, n_blocks


def _row_copy(src_hbm, row, dst, dst_row, sem):
    return pltpu.make_async_copy(src_hbm.at[pl.ds(row, 1), :], dst.at[pl.ds(dst_row, 1), :], sem)


def _gather_kernel(nused_ref, idx_ref, src_hbm, o_ref, buf, sem, *, bm):
    i = pl.program_id(0)

    @pl.when(i < nused_ref[0])
    def _():
        def issue(r, carry):
            _row_copy(src_hbm, idx_ref[0, 0, r], buf, r, sem).start()
            return carry
        lax.fori_loop(0, bm, issue, 0)
        pltpu.make_async_copy(src_hbm.at[pl.ds(0, bm), :], buf, sem).wait()
        o_ref[...] = buf[...].astype(o_ref.dtype)

    @pl.when(i >= nused_ref[0])
    def _():
        o_ref[...] = jnp.zeros_like(o_ref)


def _gather_rows(h2, buf_tok, n_used, n_blocks, bm):
    d = h2.shape[1]
    return pl.pallas_call(
        functools.partial(_gather_kernel, bm=bm),
        out_shape=jax.ShapeDtypeStruct((n_blocks * bm, d), BF16),
        grid_spec=pltpu.PrefetchScalarGridSpec(
            num_scalar_prefetch=1,
            grid=(n_blocks,),
            in_specs=[pl.BlockSpec((1, 1, bm), lambda i, nu: (i, 0, 0), memory_space=pltpu.SMEM),
                      pl.BlockSpec(memory_space=pl.ANY)],
            out_specs=pl.BlockSpec((bm, d), lambda i, nu: (i, 0)),
            scratch_shapes=[pltpu.VMEM((bm, d), F32), pltpu.SemaphoreType.DMA],
        ),
        compiler_params=_params("arbitrary"),
        name="moe_gather",
    )(n_used, buf_tok.reshape(n_blocks, 1, bm), h2)


def _expert_kernel(be_ref, nused_ref, x_ref, wg_ref, wu_ref, wd_ref, w_ref, o_ref, wg_s, wu_s, wd_s):
    i = pl.program_id(0)
    changed = jnp.logical_or(i == 0, be_ref[i] != be_ref[jnp.maximum(i - 1, 0)])

    @pl.when(i < nused_ref[0])
    def _():
        @pl.when(changed)
        def _():
            wg_s[...] = wg_ref[...].astype(BF16)
            wu_s[...] = wu_ref[...].astype(BF16)
            wd_s[...] = wd_ref[...].astype(BF16)

        x = x_ref[...]
        g = _dot(x, wg_s[...])
        u = _dot(x, wu_s[...])
        mid = (g * jax.nn.sigmoid(g) * u).astype(BF16)
        o_ref[...] = _dot(mid, wd_s[...]) * w_ref[...]

    @pl.when(i >= nused_ref[0])
    def _():
        o_ref[...] = jnp.zeros_like(o_ref)


def _experts(xs, buf_w, blk_e, n_used, w_gate, w_up, w_down, layer, n_blocks, bm):
    d = xs.shape[1]
    f = w_gate.shape[3]
    return pl.pallas_call(
        _expert_kernel,
        out_shape=jax.ShapeDtypeStruct((n_blocks * bm, d), F32),
        grid_spec=pltpu.PrefetchScalarGridSpec(
            num_scalar_prefetch=2,
            grid=(n_blocks,),
            in_specs=[
                pl.BlockSpec((bm, d), lambda i, be, nu: (i, 0)),
                pl.BlockSpec((None, None, d, f), lambda i, be, nu: (layer, be[i], 0, 0)),
                pl.BlockSpec((None, None, d, f), lambda i, be, nu: (layer, be[i], 0, 0)),
                pl.BlockSpec((None, None, f, d), lambda i, be, nu: (layer, be[i], 0, 0)),
                pl.BlockSpec((bm, 1), lambda i, be, nu: (i, 0)),
            ],
            out_specs=pl.BlockSpec((bm, d), lambda i, be, nu: (i, 0)),
            scratch_shapes=[pltpu.VMEM((d, f), BF16), pltpu.VMEM((d, f), BF16), pltpu.VMEM((f, d), BF16)],
        ),
        compiler_params=_params("arbitrary"),
        name="moe_experts",
    )(blk_e, n_used, xs, w_gate, w_up, w_down, buf_w.reshape(-1, 1))


def _combine_kernel(slot_ref, y_hbm, x_ref, g_ref, *rest, tc, final):
    if final:
        ng_ref, o_ref, buf0, buf1, sem = rest
    else:
        o_ref, buf0, buf1, sem = rest

    def issue(r, carry):
        _row_copy(y_hbm, slot_ref[0, 0, 2 * r], buf0, r, sem).start()
        _row_copy(y_hbm, slot_ref[0, 0, 2 * r + 1], buf1, r, sem).start()
        return carry
    lax.fori_loop(0, tc, issue, 0)
    pltpu.make_async_copy(y_hbm.at[pl.ds(0, tc), :], buf0, sem).wait()
    pltpu.make_async_copy(y_hbm.at[pl.ds(0, tc), :], buf1, sem).wait()
    x = x_ref[...] + g_ref[...] * (buf0[...] + buf1[...])
    if final:
        ms = jnp.mean(x * x, axis=-1, keepdims=True)
        x = x * lax.rsqrt(ms + EPS) * ng_ref[...]
    o_ref[...] = x


def _combine(yb, slot, x_all, mod3, layer, n_rows, dims, tc, norm_f_g=None):
    r, d = x_all.shape
    n_lat = dims["T"] // tc
    mm = functools.partial(_mod_row_map, layer, tc, n_lat, dims["S"], dims["B"])
    final = norm_f_g is not None
    in_specs = [
        pl.BlockSpec((1, 1, 2 * tc), lambda i: (i, 0, 0), memory_space=pltpu.SMEM),
        pl.BlockSpec(memory_space=pl.ANY),
        pl.BlockSpec((tc, d), lambda i: (i, 0)),
        pl.BlockSpec((None, 1, d), mm(5)),
    ]
    args = [slot.reshape(n_rows // tc, 1, 2 * tc), yb, x_all, mod3]
    if final:
        in_specs.append(pl.BlockSpec((1, d), lambda i: (0, 0)))
        args.append(norm_f_g.reshape(1, d))
    return pl.pallas_call(
        functools.partial(_combine_kernel, tc=tc, final=final),
        out_shape=jax.ShapeDtypeStruct((n_rows, d), F32),
        grid=(n_rows // tc,),
        in_specs=in_specs,
        out_specs=pl.BlockSpec((tc, d), lambda i: (i, 0)),
        scratch_shapes=[pltpu.VMEM((tc, d), F32), pltpu.VMEM((tc, d), F32), pltpu.SemaphoreType.DMA],
        compiler_params=_params("arbitrary"),
        name="moe_combine_residual",
    )(*args)


def _rope_table(s):
    half = ATT_DQK // 2
    rows = jnp.repeat(jnp.arange(s // GRID_W, dtype=jnp.int32), GRID_W)
    cols = jnp.tile(jnp.arange(GRID_W, dtype=jnp.int32), s // GRID_W)
    inv = ROPE_BASE ** (-jnp.arange(0, half, 2, dtype=F32) / half)
    ar = rows.astype(F32)[:, None] * inv
    ac = cols.astype(F32)[:, None] * inv
    cr, sr, cc, sc = jnp.cos(ar), jnp.sin(ar), jnp.cos(ac), jnp.sin(ac)
    z = jnp.zeros_like(sr)
    c = jnp.concatenate([cr, cr, cc, cc] * 2, axis=-1)
    s1 = jnp.concatenate([-sr, z, -sc, z] * 2, axis=-1)
    s2 = jnp.concatenate([z, sr, z, sc] * 2, axis=-1)
    return jnp.concatenate([c, s1, s2], axis=-1)


def kernel(x, c, ctx, c_ctx, w_mod, b_mod, norm1_g, norm2_g, w_in, lam_q1, lam_k1, lam_q2, lam_k2, attn_subln_g, w_attn_o, conf_dw_w, conf_dw_b, conf_ln_g, conf_ln_b, w_conf_o, hy_sc_w, hy_sc_b, hy_w1, hy_b1, hy_w2, hy_b2, hy_freq, hy_w3, hy_decay, hy_bias, w_hy_o, w_out, w_router_group, w_router_expert, w_exp_gate, w_exp_up, w_exp_down, norm_f_g):
    b, s, d = x.shape
    lc = ctx.shape[1]
    depth = w_mod.shape[0]
    t, tc_rows = b * s, b * lc
    cw, hw = conf_dw_w.shape[2], w_hy_o.shape[1]
    dims = dict(B=b, S=s, Lc=lc, T=t, QK=ATT_HEADS * 2 * ATT_DQK, AW=ATT_HEADS * ATT_DV, CW=cw, HW=hw)
    assert b + 1 <= MOD_ROWS and 2 * ATT_DQK == LANES and ATT_DV == LANES and hw == cw
    tm = _largest_tile((1024, 512, 256, 128), s, tc_rows)
    ts = _largest_tile((256, 128), s, lc)
    tn = _largest_tile((512, 256, 128), dims["QK"], dims["AW"], 2 * cw, 3 * hw, d)
    tcomb = _largest_tile((256, 128), s, tc_rows)

    x_all = jnp.concatenate([x.reshape(t, d), ctx.reshape(tc_rows, d)], axis=0)
    cond = jnp.zeros((MOD_ROWS, d), F32).at[:b].set(c).at[b].set(c_ctx)
    mod3 = _modulation(cond, w_mod, b_mod).reshape(depth * MOD_ROWS, 1, 6 * d)
    rope_tab = _rope_table(s)
    tables_lat, emb_lat = _dft_tables(s), _hyena_embedding(s)
    tables_ctx, emb_ctx = _dft_tables(lc), _hyena_embedding(lc)
    hy_params = dict(hy_w1=hy_w1, hy_b1=hy_b1, hy_w2=hy_w2, hy_b2=hy_b2, hy_freq=hy_freq, hy_w3=hy_w3,
                     hy_decay=hy_decay, hy_bias=hy_bias)

    out = None
    for l in range(depth):
        need_ctx = l < depth - 1
        n_rows = t + tc_rows if need_ctx else t
        lam_init = 0.8 - 0.6 * math.exp(-0.3 * l)
        lam = (jnp.exp(jnp.sum(lam_q1[l] * lam_k1[l])) - jnp.exp(jnp.sum(lam_q2[l] * lam_k2[l])) + lam_init)
        lam = lam.reshape(1).astype(F32)

        hl = _adanorm(x_all, norm1_g, mod3, l, dims, tm)
        qkv, conv_in, gates = _in_proj(hl, w_in, l, rope_tab, dims, tm, tn)
        att, att_ctx = _attention(qkv, lam, attn_subln_g, l, lam_init, dims, need_ctx)
        conf = _conformer(conv_in, conf_dw_w, conf_dw_b, conf_ln_g, conf_ln_b, l, dims, n_rows, ts)
        u = _hyena_short(conv_in, hy_sc_w, hy_sc_b, l, dims, n_rows, ts)
        hy = _hyena_long(u, 0, b, s, l, hy_params, tables_lat, emb_lat, dims)
        hy_ctx = _hyena_long(u, t, b, lc, l, hy_params, tables_ctx, emb_ctx, dims) if need_ctx else hy
        merged = _merge(att, att_ctx, conf, hy, hy_ctx, gates, w_attn_o, w_conf_o, w_hy_o, l, n_rows, t, d, tm, tn)
        x_all = _out_proj(merged, w_out, x_all, mod3, l, n_rows, dims, tm, tn)

        w_router = jnp.concatenate(
            [w_router_group[l], jnp.transpose(w_router_expert[l], (1, 0, 2)).reshape(d, N_EXPERTS)], axis=1)
        w_router = jnp.pad(w_router, ((0, 0), (0, LANES - w_router.shape[1])))
        h2, route = _router(x_all, norm2_g, mod3, w_router, l, n_rows, dims, tm)
        buf_tok, buf_w, slot, blk_e, n_used, n_blocks = _dispatch(route, n_rows, MOE_BLOCK)
        xs = _gather_rows(h2, buf_tok, n_used, n_blocks, MOE_BLOCK)
        yb = _experts(xs, buf_w, blk_e, n_used, w_exp_gate, w_exp_up, w_exp_down, l, n_blocks, MOE_BLOCK)
        x_new = _combine(yb, slot, x_all, mod3, l, n_rows, dims, tcomb, None if need_ctx else norm_f_g)
        if need_ctx:
            x_all = x_new
        else:
            out = x_new
    return out.reshape(b, s, d)
```

```python
import functools
import math

import jax
import jax.numpy as jnp
from jax import lax
from jax.experimental import pallas as pl
from jax.experimental.pallas import tpu as pltpu

F32 = jnp.float32
BF16 = jnp.bfloat16

GRID_W = 64
EPS = 1e-6
ATT_HEADS = 8
ATT_DQK = 64
ATT_DV = 2 * ATT_DQK
ROPE_BASE = 10000.0
CONF_K = 31
HY_ORDER = 2
HY_SHORT_K = 3
HY_EMB = 33
HY_BANDS = (HY_EMB - 1) // 2
N_BRANCH = 3
N_GROUPS = 8
E_PER_GROUP = 8
N_EXPERTS = N_GROUPS * E_PER_GROUP

LANES = 128
SUBLANES = 8
VMEM_LIMIT_BYTES = 56 * 1024 * 1024

CONF_HALO = 16
HY_HALO = 8
MOE_BLOCK = 256
ATT_UNROLL = 16
ATT_TQ = (2048, 1024, 512, 256, 128)
ATT_TK = (256, 128)
MOD_ROWS = 8


def _params(*sem):
    return pltpu.CompilerParams(dimension_semantics=sem, vmem_limit_bytes=VMEM_LIMIT_BYTES)


def _largest_tile(cands, *dims):
    for c in cands:
        if all(d % c == 0 for d in dims):
            return c
    raise ValueError(f"no tile in {cands} divides {dims}")


def _dot(a, b):
    return jnp.dot(a, b, preferred_element_type=F32)


def _split_bf16(x):
    hi = x.astype(BF16)
    lo = (x - hi.astype(F32)).astype(BF16)
    return hi, lo


def _dot3(a, b):
    ah, al = _split_bf16(a)
    bh, bl = _split_bf16(b)
    return _dot(ah, bh) + _dot(al, bh) + _dot(ah, bl)


def _mod_kernel(c_ref, w_ref, b_ref, o_ref):
    c = c_ref[...]
    s = c * jax.nn.sigmoid(c)
    o_ref[...] = _dot3(s, w_ref[...]) + b_ref[...]


def _modulation(cond, w_mod, b_mod):
    depth, d, n = w_mod.shape
    tn = _largest_tile((512, 256, 128), n)
    return pl.pallas_call(
        _mod_kernel,
        out_shape=jax.ShapeDtypeStruct((depth, MOD_ROWS, n), F32),
        grid=(depth, n // tn),
        in_specs=[
            pl.BlockSpec((MOD_ROWS, d), lambda l, j: (0, 0)),
            pl.BlockSpec((None, d, tn), lambda l, j: (l, 0, j)),
            pl.BlockSpec((None, 1, tn), lambda l, j: (l, 0, j)),
        ],
        out_specs=pl.BlockSpec((None, MOD_ROWS, tn), lambda l, j: (l, 0, j)),
        compiler_params=_params("parallel", "parallel"),
        name="modulation",
    )(cond, w_mod, b_mod.reshape(depth, 1, n))


def _ada_norm_value(x, g, shift, scale):
    ms = jnp.mean(x * x, axis=-1, keepdims=True)
    y = x * lax.rsqrt(ms + EPS) * g
    return y * (1.0 + scale) + shift


def _adanorm_kernel(x_ref, g_ref, sh_ref, sc_ref, o_ref):
    o_ref[...] = _ada_norm_value(x_ref[...], g_ref[...], sh_ref[...], sc_ref[...]).astype(o_ref.dtype)


def _mod_row_map(layer, tm, n_lat, s, b, col):
    def index_map(i, *_):
        row = jnp.where(i < n_lat, (i * tm) // s, b)
        return (layer * MOD_ROWS + row, 0, col)
    return index_map


def _adanorm(x_all, norm_g, mod3, layer, dims, tm):
    r, d = x_all.shape
    n_lat = dims["T"] // tm
    mm = functools.partial(_mod_row_map, layer, tm, n_lat, dims["S"], dims["B"])
    return pl.pallas_call(
        _adanorm_kernel,
        out_shape=jax.ShapeDtypeStruct((r, d), BF16),
        grid=(r // tm,),
        in_specs=[
            pl.BlockSpec((tm, d), lambda i: (i, 0)),
            pl.BlockSpec((None, 1, d), lambda i: (layer, 0, 0)),
            pl.BlockSpec((None, 1, d), mm(0)),
            pl.BlockSpec((None, 1, d), mm(1)),
        ],
        out_specs=pl.BlockSpec((tm, d), lambda i: (i, 0)),
        compiler_params=_params("parallel"),
        name="adanorm1",
    )(x_all, norm_g.reshape(norm_g.shape[0], 1, d), mod3, mod3)


def _qkv_kernel(a_ref, w_ref, tab_ref, o_ref, *, n_lat, tn, qk_cols):
    i = pl.program_id(0)
    j = pl.program_id(1)
    acc = _dot(a_ref[...], w_ref[...].astype(BF16))
    col0 = j * tn
    scale = jnp.where(col0 < qk_cols, ATT_DQK ** -0.5 * math.log2(math.e), 1.0).astype(F32)
    do_rope = jnp.logical_and(i < n_lat, col0 < 2 * qk_cols)

    @pl.when(do_rope)
    def _():
        c = tab_ref[:, 0:LANES]
        s1 = tab_ref[:, LANES:2 * LANES]
        s2 = tab_ref[:, 2 * LANES:3 * LANES]
        for h in range(tn // LANES):
            xh = acc[:, h * LANES:(h + 1) * LANES] * scale
            rot = xh * c + pltpu.roll(xh, LANES - 16, 1) * s1 + pltpu.roll(xh, 16, 1) * s2
            o_ref[:, h * LANES:(h + 1) * LANES] = rot.astype(o_ref.dtype)

    @pl.when(jnp.logical_not(do_rope))
    def _():
        o_ref[...] = (acc * scale).astype(o_ref.dtype)


def _plain_mm_kernel(a_ref, w_ref, o_ref):
    o_ref[...] = _dot(a_ref[...], w_ref[...].astype(BF16)).astype(o_ref.dtype)


def _sigmoid_mm_kernel(a_ref, w_ref, o_ref):
    o_ref[...] = jax.nn.sigmoid(_dot(a_ref[...], w_ref[...].astype(BF16))).astype(o_ref.dtype)


def _in_proj(hl, w_in, layer, rope_tab, dims, tm, tn):
    r, d = hl.shape
    qk, aw, cw, hw = dims["QK"], dims["AW"], dims["CW"], dims["HW"]
    n_lat = dims["T"] // tm
    s_tiles = dims["S"] // tm
    a_spec = pl.BlockSpec((tm, d), lambda i, j: (i, 0))

    def w_spec(col_off):
        off = col_off // tn
        return pl.BlockSpec((None, d, tn), lambda i, j: (layer, 0, j + off))

    def call(kernel, col_off, n_cols, dtype, name, extra_in=(), extra_specs=()):
        return pl.pallas_call(
            kernel,
            out_shape=jax.ShapeDtypeStruct((r, n_cols), dtype),
            grid=(r // tm, n_cols // tn),
            in_specs=[a_spec, w_spec(col_off), *extra_specs],
            out_specs=pl.BlockSpec((tm, tn), lambda i, j: (i, j)),
            compiler_params=_params("parallel", "arbitrary"),
            name=name,
        )(hl, w_in, *extra_in)

    qkv = call(
        functools.partial(_qkv_kernel, n_lat=n_lat, tn=tn, qk_cols=qk), 0, 2 * qk + aw, BF16, "in_proj_qkv",
        extra_in=(rope_tab,),
        extra_specs=(pl.BlockSpec((tm, 3 * LANES), lambda i, j: (i % s_tiles, 0)),))
    conv_in = call(_plain_mm_kernel, 2 * qk + aw, 2 * cw + 3 * hw, F32, "in_proj_conv")
    gates = call(_sigmoid_mm_kernel, 2 * qk + aw + 2 * cw + 3 * hw, N_BRANCH * d, BF16, "in_proj_gates")
    return qkv, conv_in, gates


def _attn_kernel(lam_ref, q_ref, g_ref, kc_ref, vc_ref, *rest, n_lat_chunks, tk, out_scale, unroll):
    if n_lat_chunks:
        kl_ref, vl_ref, o_ref, vce, vle = rest
    else:
        o_ref, vce = rest

    @pl.when(pl.program_id(2) == 0)
    def _():
        vce[:, 0:ATT_DV] = vc_ref[...]
        vce[:, ATT_DV:] = jnp.ones((vce.shape[0], ATT_DV), BF16)
        if n_lat_chunks:
            vle[:, 0:ATT_DV] = vl_ref[...]
            vle[:, ATT_DV:] = jnp.ones((vle.shape[0], ATT_DV), BF16)

    q = q_ref[...]
    lane = lax.broadcasted_iota(jnp.int32, q.shape, 1)
    zero = jnp.zeros_like(q)
    q1 = jnp.where(lane < ATT_DQK, q, zero)
    q2 = jnp.where(lane >= ATT_DQK, q, zero)
    tq = q.shape[0]
    nt = (((1,), (1,)), ((), ()))

    def one_map(qm, k, v, m, acc):
        s = lax.dot_general(qm, k, nt, preferred_element_type=F32)
        m_new = jnp.maximum(m, jnp.max(s, axis=-1, keepdims=True))
        alpha = jnp.exp2(m - m_new)
        p = jnp.exp2((s - m_new).astype(BF16))
        return m_new, alpha * acc + _dot(p, v)

    def step(k, v, carry):
        m1, a1, m2, a2 = carry
        m1, a1 = one_map(q1, k, v, m1, a1)
        m2, a2 = one_map(q2, k, v, m2, a2)
        return m1, a1, m2, a2

    neg = jnp.full((tq, 1), -jnp.inf, F32)
    za = jnp.zeros((tq, 2 * ATT_DV), F32)
    carry = step(kc_ref[...], vce[...], (neg, za, neg, za))
    if n_lat_chunks:
        def body(c, carry):
            off = pl.multiple_of(c * tk, tk)
            return step(kl_ref[pl.ds(off, tk), :], vle[pl.ds(off, tk), :], carry)
        carry = lax.fori_loop(0, n_lat_chunks, body, carry, unroll=unroll)
    _, a1, _, a2 = carry
    o = a1[:, 0:ATT_DV] / a1[:, ATT_DV:] - lam_ref[0] * (a2[:, 0:ATT_DV] / a2[:, ATT_DV:])
    ms = jnp.mean(o * o, axis=-1, keepdims=True)
    o = o * lax.rsqrt(ms + 1e-5) * g_ref[...] * out_scale
    o_ref[...] = o.astype(o_ref.dtype)


def _attention(qkv, lam, subln_g, layer, lam_init, dims, with_ctx):
    b, s, lc, t = dims["B"], dims["S"], dims["Lc"], dims["T"]
    h = ATT_HEADS
    tq = _largest_tile(ATT_TQ, s)
    tk = _largest_tile(ATT_TK, s)
    tqc = _largest_tile((512, 256, 128), lc)
    smem = pl.BlockSpec(memory_space=pltpu.SMEM)
    g3 = subln_g.reshape(subln_g.shape[0], 1, ATT_DV)
    g_spec = pl.BlockSpec((None, 1, ATT_DV), lambda bi, hi, ti: (layer, 0, 0))
    kc_spec = pl.BlockSpec((lc, LANES), lambda bi, hi, ti: (t // lc + bi, h + hi))
    vc_spec = pl.BlockSpec((lc, LANES), lambda bi, hi, ti: (t // lc + bi, 2 * h + hi))
    out_scale = 1.0 - lam_init

    sem = ("arbitrary", "arbitrary", "arbitrary")
    att = pl.pallas_call(
        functools.partial(_attn_kernel, n_lat_chunks=s // tk, tk=tk, out_scale=out_scale,
                          unroll=min(ATT_UNROLL, s // tk)),
        out_shape=jax.ShapeDtypeStruct((t, h * ATT_DV), BF16),
        grid=(b, h, s // tq),
        in_specs=[
            smem,
            pl.BlockSpec((tq, LANES), lambda bi, hi, ti: (bi * (s // tq) + ti, hi)),
            g_spec, kc_spec, vc_spec,
            pl.BlockSpec((s, LANES), lambda bi, hi, ti: (bi, h + hi)),
            pl.BlockSpec((s, LANES), lambda bi, hi, ti: (bi, 2 * h + hi)),
        ],
        out_specs=pl.BlockSpec((tq, LANES), lambda bi, hi, ti: (bi * (s // tq) + ti, hi)),
        scratch_shapes=[pltpu.VMEM((lc, 2 * ATT_DV), BF16), pltpu.VMEM((s, 2 * ATT_DV), BF16)],
        compiler_params=_params(*sem),
        name="diff_attn_latent",
    )(lam, qkv, g3, qkv, qkv, qkv, qkv)
    if not with_ctx:
        return att, att

    att_ctx = pl.pallas_call(
        functools.partial(_attn_kernel, n_lat_chunks=0, tk=tk, out_scale=out_scale, unroll=1),
        out_shape=jax.ShapeDtypeStruct((b * lc, h * ATT_DV), BF16),
        grid=(b, h, lc // tqc),
        in_specs=[
            smem,
            pl.BlockSpec((tqc, LANES), lambda bi, hi, ti: ((t + bi * lc) // tqc + ti, hi)),
            g_spec, kc_spec, vc_spec,
        ],
        out_specs=pl.BlockSpec((tqc, LANES), lambda bi, hi, ti: (bi * (lc // tqc) + ti, hi)),
        scratch_shapes=[pltpu.VMEM((lc, 2 * ATT_DV), BF16)],
        compiler_params=_params(*sem),
        name="diff_attn_ctx",
    )(lam, qkv, g3, qkv, qkv)
    return att, att_ctx


def _seq_edges(i, n_lat, tps_lat, tps_ctx):
    is_lat = i < n_lat
    pos = jnp.where(is_lat, i % tps_lat, (i - n_lat) % tps_ctx)
    n = jnp.where(is_lat, tps_lat, tps_ctx)
    return pos == 0, pos == n - 1


def _halo_specs(ts, halo, width, col, n_rows):
    per = ts // halo
    last = n_rows // halo - 1
    prev = pl.BlockSpec((halo, width), lambda i, *_: (jnp.maximum(i * per - 1, 0), col(*_)))
    cur = pl.BlockSpec((ts, width), lambda i, *_: (i, col(*_)))
    nxt = pl.BlockSpec((halo, width), lambda i, *_: (jnp.minimum((i + 1) * per, last), col(*_)))
    return prev, cur, nxt


def _conf_kernel(ap_ref, ac_ref, an_ref, gp_ref, gc_ref, gn_ref, w_ref, b_ref, lg_ref, lb_ref, o_ref, ybuf,
                 *, ts, n_lat, tps_lat, tps_ctx, row_chunk):
    first, last = _seq_edges(pl.program_id(0), n_lat, tps_lat, tps_ctx)
    halo = CONF_HALO

    def glu(a, g):
        return a * jax.nn.sigmoid(g)

    prev = glu(ap_ref[...], gp_ref[...])
    nxt = glu(an_ref[...], gn_ref[...])
    ybuf[0:halo, :] = jnp.where(first, jnp.zeros_like(prev), prev)
    ybuf[halo:halo + ts, :] = glu(ac_ref[...], gc_ref[...])
    ybuf[halo + ts:, :] = jnp.where(last, jnp.zeros_like(nxt), nxt)
    base = halo - CONF_K // 2
    for r0 in range(0, ts, row_chunk):
        acc = jnp.zeros((row_chunk, ybuf.shape[1]), F32) + b_ref[...]
        for j in range(CONF_K):
            acc = acc + w_ref[j:j + 1, :] * ybuf[r0 + base + j:r0 + base + j + row_chunk, :]
        mu = jnp.mean(acc, axis=-1, keepdims=True)
        xc = acc - mu
        var = jnp.mean(xc * xc, axis=-1, keepdims=True)
        y = xc * lax.rsqrt(var + 1e-5) * lg_ref[...] + lb_ref[...]
        o_ref[r0:r0 + row_chunk, :] = (y * jax.nn.sigmoid(y)).astype(o_ref.dtype)


def _conformer(conv_in, dw_w, dw_b, ln_g, ln_b, layer, dims, n_rows, ts):
    cw = dims["CW"]
    depth = dw_w.shape[0]
    n_lat = dims["T"] // ts
    a_specs = _halo_specs(ts, CONF_HALO, cw, lambda: 0, conv_in.shape[0])
    g_specs = _halo_specs(ts, CONF_HALO, cw, lambda: 1, conv_in.shape[0])
    vec = lambda: pl.BlockSpec((None, 1, cw), lambda i: (layer, 0, 0))
    kern = functools.partial(_conf_kernel, ts=ts, n_lat=n_lat, tps_lat=dims["S"] // ts, tps_ctx=dims["Lc"] // ts,
                             row_chunk=min(32, ts))
    return pl.pallas_call(
        kern,
        out_shape=jax.ShapeDtypeStruct((n_rows, cw), BF16),
        grid=(n_rows // ts,),
        in_specs=[*a_specs, *g_specs,
                  pl.BlockSpec((None, CONF_K, cw), lambda i: (layer, 0, 0)), vec(), vec(), vec()],
        out_specs=pl.BlockSpec((ts, cw), lambda i: (i, 0)),
        scratch_shapes=[pltpu.VMEM((ts + 2 * CONF_HALO, cw), F32)],
        compiler_params=_params("parallel"),
        name="conformer_conv",
    )(conv_in, conv_in, conv_in, conv_in, conv_in, conv_in, dw_w,
      dw_b.reshape(depth, 1, cw), ln_g.reshape(depth, 1, cw), ln_b.reshape(depth, 1, cw))


def _hy_short_kernel(p_ref, c_ref, n_ref, w_ref, b_ref, o_ref, ybuf, *, ts, n_lat, tps_lat, tps_ctx):
    first, last = _seq_edges(pl.program_id(0), n_lat, tps_lat, tps_ctx)
    halo = HY_HALO
    prev = p_ref[...]
    nxt = n_ref[...]
    ybuf[0:halo, :] = jnp.where(first, jnp.zeros_like(prev), prev)
    ybuf[halo:halo + ts, :] = c_ref[...]
    ybuf[halo + ts:, :] = jnp.where(last, jnp.zeros_like(nxt), nxt)
    acc = b_ref[...] + w_ref[0:1, :] * ybuf[halo - 1:halo - 1 + ts, :]
    acc = acc + w_ref[1:2, :] * ybuf[halo:halo + ts, :]
    acc = acc + w_ref[2:3, :] * ybuf[halo + 1:halo + 1 + ts, :]
    o_ref[...] = acc


def _hyena_short(conv_in, sc_w, sc_b, layer, dims, n_rows, ts):
    cw, hw = dims["CW"], dims["HW"]
    depth = sc_w.shape[0]
    n_lat = dims["T"] // ts
    col0 = 2 * cw // hw
    specs = _halo_specs(ts, HY_HALO, hw, lambda m: col0 + m, conv_in.shape[0])
    kern = functools.partial(_hy_short_kernel, ts=ts, n_lat=n_lat, tps_lat=dims["S"] // ts,
                             tps_ctx=dims["Lc"] // ts)
    return pl.pallas_call(
        kern,
        out_shape=jax.ShapeDtypeStruct((n_rows, 3 * hw), F32),
        grid=(n_rows // ts, 3),
        in_specs=[*specs,
                  pl.BlockSpec((None, HY_SHORT_K, hw), lambda i, m: (layer, 0, m)),
                  pl.BlockSpec((None, 1, hw), lambda i, m: (layer, 0, m))],
        out_specs=pl.BlockSpec((ts, hw), lambda i, m: (i, m)),
        scratch_shapes=[pltpu.VMEM((ts + 2 * HY_HALO, hw), F32)],
        compiler_params=_params("parallel", "arbitrary"),
        name="hyena_short_conv",
    )(conv_in, conv_in, conv_in, sc_w, sc_b.reshape(depth, 1, 3 * hw))


def _filter_kernel(emb_ref, w1_ref, b1_ref, f_ref, w2_ref, b2_ref, w3_ref, dec_ref, h_ref, asum_ref, *, hw, tl):
    i = pl.program_id(0)
    hp = lax.Precision.HIGHEST
    emb = emb_ref[...]
    t = emb[:, 0:1]
    h1 = jnp.sin(f_ref[0:1, :] * (jnp.dot(emb, w1_ref[...], precision=hp, preferred_element_type=F32) + b1_ref[...]))
    h2 = jnp.sin(f_ref[1:2, :] * (jnp.dot(h1, w2_ref[...], precision=hp, preferred_element_type=F32) + b2_ref[...]))
    h = jnp.dot(h2, w3_ref[...], precision=hp, preferred_element_type=F32)
    h = h * jnp.exp(-t * jnp.abs(dec_ref[...]))
    col = lax.broadcasted_iota(jnp.int32, h.shape, 1)
    row = lax.broadcasted_iota(jnp.int32, h.shape, 0) + i * tl
    backward = (col // hw) % 2 == 1
    h = jnp.where(jnp.logical_and(backward, row == 0), 0.0, h)
    h_ref[...] = h.astype(h_ref.dtype)

    @pl.when(i == 0)
    def _():
        asum_ref[...] = jnp.zeros_like(asum_ref)

    part = jnp.abs(h).reshape(tl // SUBLANES, SUBLANES, h.shape[1]).sum(axis=0)
    asum_ref[...] += part


def _hyena_filter(emb, w1, b1, freq, w2, b2, w3, decay, layer, hw):
    l, ke = emb.shape
    depth, _, ffn = w1.shape
    nf = w3.shape[2]
    tl = _largest_tile((512, 256, 128), l)
    w1p = jnp.pad(w1, ((0, 0), (0, ke - w1.shape[1]), (0, 0)))
    vec = lambda n: pl.BlockSpec((None, 1, n), lambda i: (layer, 0, 0))
    return pl.pallas_call(
        functools.partial(_filter_kernel, hw=hw, tl=tl),
        out_shape=(jax.ShapeDtypeStruct((l, nf), BF16), jax.ShapeDtypeStruct((SUBLANES, nf), F32)),
        grid=(l // tl,),
        in_specs=[
            pl.BlockSpec((tl, ke), lambda i: (i, 0)),
            pl.BlockSpec((None, ke, ffn), lambda i: (layer, 0, 0)),
            vec(ffn),
            pl.BlockSpec((None, 2, ffn), lambda i: (layer, 0, 0)),
            pl.BlockSpec((None, ffn, ffn), lambda i: (layer, 0, 0)),
            vec(ffn),
            pl.BlockSpec((None, ffn, nf), lambda i: (layer, 0, 0)),
            vec(nf),
        ],
        out_specs=(pl.BlockSpec((tl, nf), lambda i: (i, 0)), pl.BlockSpec((SUBLANES, nf), lambda i: (0, 0))),
        compiler_params=_params("arbitrary"),
        name="hyena_filter_mlp",
    )(emb, w1p, b1.reshape(depth, 1, ffn), freq, w2, b2.reshape(depth, 1, ffn), w3, decay.reshape(depth, 1, nf))


def _bin_scale(i, tm, n_bins):
    row = lax.broadcasted_iota(jnp.int32, (tm, 1), 0) + i * tm
    is0 = row == 0
    return is0, jnp.where(is0, 1.0 / (2 * n_bins), 2.0 / (2 * n_bins)).astype(F32)


def _filter_dft_kernel(c_ref, s_ref, hf_ref, hb_ref, af_ref, ab_ref, kre_ref, kim_ref, *, tm, n_bins):
    i = pl.program_id(1)
    c = c_ref[...]
    s = s_ref[...]
    fre, fim = _dot(c, hf_ref[...]), _dot(s, hf_ref[...])
    bre, bim = _dot(c, hb_ref[...]), _dot(s, hb_ref[...])
    norm = jnp.sum(af_ref[...], axis=0, keepdims=True) + jnp.sum(ab_ref[...], axis=0, keepdims=True)
    is0, cs = _bin_scale(i, tm, n_bins)
    w = cs / norm
    kre_ref[...] = (fre + bre) * w
    kim_ref[...] = jnp.where(is0, fim + bim, fim - bim) * w


def _filter_spectrum(h, asum, c_tab, s_tab, hw):
    l, nf = h.shape
    n_ord = nf // (2 * hw)
    tm = _largest_tile((512, 256, 128), l)
    out = jax.ShapeDtypeStruct((l, n_ord * hw), F32)
    tab = pl.BlockSpec((tm, l), lambda o, i: (i, 0))
    return pl.pallas_call(
        functools.partial(_filter_dft_kernel, tm=tm, n_bins=l),
        out_shape=(out, out),
        grid=(n_ord, l // tm),
        in_specs=[tab, tab,
                  pl.BlockSpec((l, hw), lambda o, i: (0, 2 * o)),
                  pl.BlockSpec((l, hw), lambda o, i: (0, 2 * o + 1)),
                  pl.BlockSpec((SUBLANES, hw), lambda o, i: (0, 2 * o)),
                  pl.BlockSpec((SUBLANES, hw), lambda o, i: (0, 2 * o + 1))],
        out_specs=(pl.BlockSpec((tm, hw), lambda o, i: (i, o)), pl.BlockSpec((tm, hw), lambda o, i: (i, o))),
        compiler_params=_params("parallel", "arbitrary"),
        name="hyena_filter_dft",
    )(c_tab, s_tab, h, h, asum, asum)


def _fwd_dft_kernel(c_ref, s_ref, z_ref, kre_ref, kim_ref, yre_ref, yim_ref, zb, *, tm):
    i = pl.program_id(1)

    @pl.when(i == 0)
    def _():
        zb[...] = z_ref[...].astype(BF16)

    xre = _dot(c_ref[...], zb[...])
    xim = _dot(s_ref[...], zb[...])
    kre, kim = kre_ref[...], kim_ref[...]
    is0 = (lax.broadcasted_iota(jnp.int32, (tm, 1), 0) + i * tm) == 0
    yre = xre * kre - jnp.where(is0, 0.0, xim * kim)
    yim = jnp.where(is0, xim * kim, xre * kim + xim * kre)
    yre_ref[...] = yre.astype(BF16)
    yim_ref[...] = yim.astype(BF16)


def _fwd_dft(z_arr, z_row0, z_col, kre, kim, order, c_tab, s_tab, b, l, hw):
    tm = _largest_tile((512, 256, 128), l)
    out = jax.ShapeDtypeStruct((b * l, hw), BF16)
    tab = pl.BlockSpec((tm, l), lambda bi, i: (i, 0))
    kspec = pl.BlockSpec((tm, hw), lambda bi, i: (i, order))
    ospec = pl.BlockSpec((tm, hw), lambda bi, i: (bi * (l // tm) + i, 0))
    return pl.pallas_call(
        functools.partial(_fwd_dft_kernel, tm=tm),
        out_shape=(out, out),
        grid=(b, l // tm),
        in_specs=[tab, tab, pl.BlockSpec((l, hw), lambda bi, i: (z_row0 // l + bi, z_col)), kspec, kspec],
        out_specs=(ospec, ospec),
        scratch_shapes=[pltpu.VMEM((l, hw), BF16)],
        compiler_params=_params("parallel", "arbitrary"),
        name="hyena_fwd_dft",
    )(c_tab, s_tab, z_arr, kre, kim)


def _inv_dft_kernel(c_ref, st_ref, yre_ref, yim_ref, gate_ref, z_ref, bias_ref, o_ref):
    y = _dot(c_ref[...], yre_ref[...]) + _dot(st_ref[...], yim_ref[...])
    o_ref[...] = (gate_ref[...] * (y + bias_ref[...] * z_ref[...])).astype(o_ref.dtype)


def _inv_dft(yre, yim, u, u_row0, gate_col, z_arr, z_row0, z_col, bias, layer, order, c_tab, st_tab, b, l, hw,
             out_dtype):
    tm = _largest_tile((512, 256, 128), l)
    per = l // tm
    tab = pl.BlockSpec((tm, l), lambda bi, i: (i, 0))
    yspec = pl.BlockSpec((l, hw), lambda bi, i: (bi, 0))
    return pl.pallas_call(
        _inv_dft_kernel,
        out_shape=jax.ShapeDtypeStruct((b * l, hw), out_dtype),
        grid=(b, per),
        in_specs=[tab, tab, yspec, yspec,
                  pl.BlockSpec((tm, hw), lambda bi, i: (u_row0 // tm + bi * per + i, gate_col)),
                  pl.BlockSpec((tm, hw), lambda bi, i: (z_row0 // tm + bi * per + i, z_col)),
                  pl.BlockSpec((None, None, 1, hw), lambda bi, i: (layer, order, 0, 0))],
        out_specs=pl.BlockSpec((tm, hw), lambda bi, i: (bi * per + i, 0)),
        compiler_params=_params("parallel", "arbitrary"),
        name="hyena_inv_dft",
    )(c_tab, st_tab, yre, yim, u, z_arr, bias.reshape(bias.shape[0], bias.shape[1], 1, hw))


def _dft_table_kernel(c_ref, s_ref, st_ref, *, tm, l):
    n = 2 * l
    w = 2.0 * math.pi / n
    row = lax.broadcasted_iota(jnp.int32, (tm, LANES), 0) + pl.program_id(0) * tm
    lane = lax.broadcasted_iota(jnp.int32, (tm, LANES), 1)
    fine = ((row * lane) & (n - 1)).astype(F32) * w
    coarse = ((row * (lane * LANES)) & (n - 1)).astype(F32) * w
    cb, sb = jnp.cos(fine), jnp.sin(fine)
    ca, sa = jnp.cos(coarse), jnp.sin(coarse)
    alt_row = (1 - 2 * (row & 1)).astype(F32)
    for j in range(l // LANES):
        caj = jnp.broadcast_to(ca[:, j:j + 1], (tm, LANES))
        saj = jnp.broadcast_to(sa[:, j:j + 1], (tm, LANES))
        col = lane + j * LANES
        cos_blk = caj * cb - saj * sb
        nsin_blk = -(saj * cb + caj * sb)
        alt_col = (1 - 2 * (col & 1)).astype(F32)
        sl = slice(j * LANES, (j + 1) * LANES)
        c_ref[:, sl] = cos_blk.astype(BF16)
        s_ref[:, sl] = jnp.where(row == 0, alt_col, nsin_blk).astype(BF16)
        st_ref[:, sl] = jnp.where(col == 0, alt_row, nsin_blk).astype(BF16)


def _dft_tables(l):
    assert l % LANES == 0 and (l & (l - 1)) == 0 and l // LANES <= LANES
    tm = _largest_tile((256, 128), l)
    out = jax.ShapeDtypeStruct((l, l), BF16)
    spec = pl.BlockSpec((tm, l), lambda i: (i, 0))
    return pl.pallas_call(
        functools.partial(_dft_table_kernel, tm=tm, l=l),
        out_shape=(out, out, out),
        grid=(l // tm,),
        out_specs=(spec, spec, spec),
        compiler_params=_params("parallel"),
        name="dft_tables",
    )()


def _hyena_embedding(l):
    n = jnp.arange(l, dtype=F32)
    t = n / max(l - 1, 1)
    w = 2.0 * math.pi * n / l
    f = jnp.linspace(1e-4, HY_BANDS - 1, HY_BANDS, dtype=F32)
    fw = w[:, None] * f[None, :]
    emb = jnp.concatenate([t[:, None], jnp.cos(fw), -jnp.sin(fw)], axis=-1)
    return jnp.pad(emb, ((0, 0), (0, LANES - HY_EMB)))


def _hyena_long(u, u_row0, b, l, layer, p, tables, emb, dims):
    hw = dims["HW"]
    c_tab, s_tab, st_tab = tables
    h, asum = _hyena_filter(emb, p["hy_w1"], p["hy_b1"], p["hy_freq"], p["hy_w2"], p["hy_b2"], p["hy_w3"],
                            p["hy_decay"], layer, hw)
    kre, kim = _filter_spectrum(h, asum, c_tab, s_tab, hw)
    yre, yim = _fwd_dft(u, u_row0, 2, kre, kim, 0, c_tab, s_tab, b, l, hw)
    z1 = _inv_dft(yre, yim, u, u_row0, 0, u, u_row0, 2, p["hy_bias"], layer, 0, c_tab, st_tab, b, l, hw, F32)
    yre, yim = _fwd_dft(z1, 0, 0, kre, kim, 1, c_tab, s_tab, b, l, hw)
    return _inv_dft(yre, yim, u, u_row0, 1, z1, 0, 0, p["hy_bias"], layer, 1, c_tab, st_tab, b, l, hw, BF16)


def _merge_kernel(attl_ref, attc_ref, conf_ref, hyl_ref, hyc_ref, wa_ref, wc_ref, wh_ref, g0_ref, g1_ref, g2_ref,
                  o_ref, *, n_lat):
    is_lat = pl.program_id(0) < n_lat
    a = _dot(jnp.where(is_lat, attl_ref[...], attc_ref[...]), wa_ref[...].astype(BF16))
    c = _dot(conf_ref[...], wc_ref[...].astype(BF16))
    h = _dot(jnp.where(is_lat, hyl_ref[...], hyc_ref[...]), wh_ref[...].astype(BF16))
    m = g0_ref[...].astype(F32) * a + g1_ref[...].astype(F32) * c + g2_ref[...].astype(F32) * h
    o_ref[...] = m.astype(o_ref.dtype)


def _merge(att, att_ctx, conf, hy, hy_ctx, gates, w_attn_o, w_conf_o, w_hy_o, layer, n_rows, n_lat_rows, d, tm, tn):
    aw, cw, hw = att.shape[1], conf.shape[1], hy.shape[1]
    gb = d // tn
    n_lat = n_lat_rows // tm
    lat_map = lambda i, j: (jnp.minimum(i, n_lat - 1), 0)
    ctx_map = lambda i, j: (jnp.maximum(i - n_lat, 0), 0)
    return pl.pallas_call(
        functools.partial(_merge_kernel, n_lat=n_lat),
        out_shape=jax.ShapeDtypeStruct((n_rows, d), BF16),
        grid=(n_rows // tm, d // tn),
        in_specs=[
            pl.BlockSpec((tm, aw), lat_map),
            pl.BlockSpec((tm, aw), ctx_map),
            pl.BlockSpec((tm, cw), lambda i, j: (i, 0)),
            pl.BlockSpec((tm, hw), lat_map),
            pl.BlockSpec((tm, hw), ctx_map),
            pl.BlockSpec((None, aw, tn), lambda i, j: (layer, 0, j)),
            pl.BlockSpec((None, cw, tn), lambda i, j: (layer, 0, j)),
            pl.BlockSpec((None, hw, tn), lambda i, j: (layer, 0, j)),
            pl.BlockSpec((tm, tn), lambda i, j: (i, j)),
            pl.BlockSpec((tm, tn), lambda i, j: (i, gb + j)),
            pl.BlockSpec((tm, tn), lambda i, j: (i, 2 * gb + j)),
        ],
        out_specs=pl.BlockSpec((tm, tn), lambda i, j: (i, j)),
        compiler_params=_params("parallel", "arbitrary"),
        name="branch_proj_merge",
    )(att, att_ctx, conf, hy, hy_ctx, w_attn_o, w_conf_o, w_hy_o, gates, gates, gates)


def _out_proj_kernel(a_ref, w_ref, x_ref, g_ref, o_ref):
    o_ref[...] = x_ref[...] + g_ref[...] * _dot(a_ref[...], w_ref[...].astype(BF16))


def _out_proj(merged, w_out, x_all, mod3, layer, n_rows, dims, tm, tn):
    d = x_all.shape[1]
    n_lat = dims["T"] // tm
    mm = functools.partial(_mod_row_map, layer, tm, n_lat, dims["S"], dims["B"])

    def gate_map(i, j):
        row, z, _ = mm(0)(i)
        return (row, z, 2 * (d // tn) + j)

    return pl.pallas_call(
        _out_proj_kernel,
        out_shape=jax.ShapeDtypeStruct((n_rows, d), F32),
        grid=(n_rows // tm, d // tn),
        in_specs=[
            pl.BlockSpec((tm, d), lambda i, j: (i, 0)),
            pl.BlockSpec((None, d, tn), lambda i, j: (layer, 0, j)),
            pl.BlockSpec((tm, tn), lambda i, j: (i, j)),
            pl.BlockSpec((None, 1, tn), gate_map),
        ],
        out_specs=pl.BlockSpec((tm, tn), lambda i, j: (i, j)),
        compiler_params=_params("parallel", "arbitrary"),
        name="out_proj_residual",
    )(merged, w_out, x_all, mod3)


def _store_token_tiled(ref, x):
    n, d = x.shape
    ns = d // LANES
    for s in range(ns):
        ref[pl.ds(s, n, stride=ns), :] = x[:, s * LANES:(s + 1) * LANES]


def _router_kernel(x_ref, g_ref, sh_ref, sc_ref, wr_ref, h_ref, r_ref):
    h = _ada_norm_value(x_ref[...], g_ref[...], sh_ref[...], sc_ref[...])
    _store_token_tiled(h_ref, h)
    logits = _dot3(h, wr_ref[...])
    lane = lax.broadcasted_iota(jnp.int32, logits.shape, 1).astype(F32)
    ninf = jnp.float32(-jnp.inf)
    big = jnp.float32(LANES)

    def first_argmax(v):
        mx = jnp.max(v, axis=-1, keepdims=True)
        idx = jnp.min(jnp.where(v == mx, lane, big), axis=-1, keepdims=True)
        return mx, idx

    lg = jnp.where(lane < N_GROUPS, logits, ninf)
    gmax, gidx = first_argmax(lg)
    p_grp = 1.0 / jnp.sum(jnp.exp(lg - gmax), axis=-1, keepdims=True)
    lo = N_GROUPS + gidx * E_PER_GROUP
    le = jnp.where(jnp.logical_and(lane >= lo, lane < lo + E_PER_GROUP), logits, ninf)
    e1, i1 = first_argmax(le)
    e2, i2 = first_argmax(jnp.where(lane == i1, ninf, le))
    t = jnp.exp(e2 - e1)
    w1 = p_grp / (1.0 + t)
    w2 = p_grp * t / (1.0 + t)
    out = jnp.where(lane == 0, i1 - N_GROUPS,
                    jnp.where(lane == 1, i2 - N_GROUPS,
                              jnp.where(lane == 2, w1, jnp.where(lane == 3, w2, 0.0))))
    r_ref[...] = out


def _router(x_all, norm_g, mod3, w_router, layer, n_rows, dims, tm):
    d = x_all.shape[1]
    ns = d // LANES
    n_lat = dims["T"] // tm
    mm = functools.partial(_mod_row_map, layer, tm, n_lat, dims["S"], dims["B"])
    return pl.pallas_call(
        _router_kernel,
        out_shape=(jax.ShapeDtypeStruct((n_rows * ns, LANES), F32), jax.ShapeDtypeStruct((n_rows, LANES), F32)),
        grid=(n_rows // tm,),
        in_specs=[
            pl.BlockSpec((tm, d), lambda i: (i, 0)),
            pl.BlockSpec((None, 1, d), lambda i: (layer, 0, 0)),
            pl.BlockSpec((None, 1, d), mm(3)),
            pl.BlockSpec((None, 1, d), mm(4)),
            pl.BlockSpec((d, LANES), lambda i: (0, 0)),
        ],
        out_specs=(pl.BlockSpec((tm * ns, LANES), lambda i: (i, 0)), pl.BlockSpec((tm, LANES), lambda i: (i, 0))),
        compiler_params=_params("parallel"),
        name="adanorm2_router",
    )(x_all, norm_g.reshape(norm_g.shape[0], 1, d), mod3, mod3, w_router)


def _dispatch(route, n_tok, bm):
    eid = route[:, 0:2].astype(jnp.int32).reshape(-1)
    a = 2 * n_tok
    e_s, order = lax.sort_key_val(eid, jnp.arange(a, dtype=jnp.int32))
    experts = jnp.arange(N_EXPERTS, dtype=jnp.int32)
    cstart = jnp.searchsorted(e_s, experts, side='left').astype(jnp.int32)
    counts = jnp.searchsorted(e_s, experts, side='right').astype(jnp.int32) - cstart
    pcounts = (counts + bm - 1) // bm * bm
    pend = jnp.cumsum(pcounts)
    pstart = pend - pcounts
    n_blocks = -(-a // bm) + N_EXPERTS
    blk = jnp.arange(n_blocks, dtype=jnp.int32)
    blk_e = jnp.minimum(jnp.searchsorted(pend, blk * bm, side='right'), N_EXPERTS - 1).astype(jnp.int32)
    k = blk - pstart[blk_e] // bm
    blk_src0 = jnp.clip(cstart[blk_e] + k * bm, 0, a - 1).astype(jnp.int32)
    blk_nvalid = jnp.clip(counts[blk_e] - k * bm, 0, bm).astype(jnp.int32)
    n_used = (pend[-1] // bm).astype(jnp.int32).reshape(1)
    return order, blk_e, blk_src0, blk_nvalid, n_used, n_blocks


def _token_copy(src, src_tok, dst, dst_tok, ns, sem):
    return pltpu.make_async_copy(src.at[pl.ds(pl.multiple_of(src_tok * ns, ns), ns), :],
                                 dst.at[pl.ds(pl.multiple_of(dst_tok * ns, ns), ns), :], sem)


def _gather_kernel(order_ref, src0_ref, nused_ref, src_hbm, o_ref, buf, sem, *, bm, ns, n_assign):
    i = pl.program_id(0)

    @pl.when(i < nused_ref[0])
    def _():
        base = src0_ref[i]

        def issue(r, carry):
            tok = order_ref[jnp.minimum(base + r, n_assign - 1)] // 2
            _token_copy(src_hbm, tok, buf, r, ns, sem).start()
            return carry
        lax.fori_loop(0, bm, issue, 0, unroll=8)
        pltpu.make_async_copy(src_hbm.at[pl.ds(0, bm * ns), :], buf, sem).wait()
        for s in range(ns):
            o_ref[:, s * LANES:(s + 1) * LANES] = buf[pl.ds(s, bm, stride=ns), :].astype(o_ref.dtype)

    @pl.when(i >= nused_ref[0])
    def _():
        o_ref[...] = jnp.zeros_like(o_ref)


def _gather_rows(h2t, order, blk_src0, n_used, n_blocks, bm, d):
    ns = d // LANES
    return pl.pallas_call(
        functools.partial(_gather_kernel, bm=bm, ns=ns, n_assign=order.shape[0]),
        out_shape=jax.ShapeDtypeStruct((n_blocks * bm, d), BF16),
        grid_spec=pltpu.PrefetchScalarGridSpec(
            num_scalar_prefetch=3,
            grid=(n_blocks,),
            in_specs=[pl.BlockSpec(memory_space=pl.ANY)],
            out_specs=pl.BlockSpec((bm, d), lambda i, *_: (i, 0)),
            scratch_shapes=[pltpu.VMEM((bm * ns, LANES), F32), pltpu.SemaphoreType.DMA],
        ),
        compiler_params=_params("arbitrary"),
        name="moe_gather",
    )(order, blk_src0, n_used, h2t)


def _expert_kernel(order_ref, be_ref, src0_ref, nvalid_ref, nused_ref, x_ref, wg_ref, wu_ref, wd_ref, y_hbm,
                   wg_s, wu_s, wd_s, obuf, sem, *, bm, ns, n_assign, n_blocks):
    i = pl.program_id(0)
    slot = i % 2
    changed = jnp.logical_or(i == 0, be_ref[i] != be_ref[jnp.maximum(i - 1, 0)])

    def wait_slot(sl):
        pltpu.make_async_copy(obuf.at[sl], y_hbm.at[pl.ds(0, bm * ns), :], sem.at[sl]).wait()

    @pl.when(i >= 2)
    def _():
        wait_slot(slot)

    @pl.when(i < nused_ref[0])
    def _():
        @pl.when(changed)
        def _():
            wg_s[...] = wg_ref[...].astype(BF16)
            wu_s[...] = wu_ref[...].astype(BF16)
            wd_s[...] = wd_ref[...].astype(BF16)

        x = x_ref[...]
        g = _dot(x, wg_s[...])
        u = _dot(x, wu_s[...])
        mid = (g * jax.nn.sigmoid(g) * u).astype(BF16)
        _store_token_tiled(obuf.at[slot], _dot(mid, wd_s[...]))

    @pl.when(i >= nused_ref[0])
    def _():
        obuf[slot] = jnp.zeros(obuf.shape[1:], obuf.dtype)

    base = src0_ref[i]
    n_real = nvalid_ref[i]

    def issue(r, carry):
        real = order_ref[jnp.minimum(base + r, n_assign - 1)]
        spare = n_assign + slot * bm + r
        _token_copy(obuf.at[slot], r, y_hbm, jnp.where(r < n_real, real, spare), ns, sem.at[slot]).start()
        return carry
    lax.fori_loop(0, bm, issue, 0, unroll=8)

    @pl.when(i == n_blocks - 1)
    def _():
        if n_blocks > 1:
            wait_slot(1 - slot)
        wait_slot(slot)


def _experts(xs, order, blk_e, blk_src0, blk_nvalid, n_used, w_gate, w_up, w_down, layer, n_blocks, bm):
    d = xs.shape[1]
    f = w_gate.shape[3]
    ns = d // LANES
    n_assign = order.shape[0]
    wspec = lambda shape: pl.BlockSpec((None, None) + shape, lambda i, o, be, *_: (layer, be[i], 0, 0))
    return pl.pallas_call(
        functools.partial(_expert_kernel, bm=bm, ns=ns, n_assign=n_assign, n_blocks=n_blocks),
        out_shape=jax.ShapeDtypeStruct(((n_assign + 2 * bm) * ns, LANES), F32),
        grid_spec=pltpu.PrefetchScalarGridSpec(
            num_scalar_prefetch=5,
            grid=(n_blocks,),
            in_specs=[pl.BlockSpec((bm, d), lambda i, *_: (i, 0)), wspec((d, f)), wspec((d, f)), wspec((f, d))],
            out_specs=pl.BlockSpec(memory_space=pl.ANY),
            scratch_shapes=[pltpu.VMEM((d, f), BF16), pltpu.VMEM((d, f), BF16), pltpu.VMEM((f, d), BF16),
                            pltpu.VMEM((2, bm * ns, LANES), F32), pltpu.SemaphoreType.DMA((2,))],
        ),
        compiler_params=_params("arbitrary"),
        name="moe_experts",
    )(order, blk_e, blk_src0, blk_nvalid, n_used, xs, w_gate, w_up, w_down)


def _combine_kernel(y_ref, x_ref, g_ref, route_ref, *rest, tc, ns, final):
    if final:
        ng_ref, o_ref = rest
    else:
        (o_ref,) = rest
    w0 = route_ref[:, 2:3]
    w1 = route_ref[:, 3:4]
    ssq = jnp.zeros((tc, 1), F32)
    for s in range(ns):
        sl = slice(s * LANES, (s + 1) * LANES)
        y = w0 * y_ref[pl.ds(s, tc, stride=2 * ns), :] + w1 * y_ref[pl.ds(ns + s, tc, stride=2 * ns), :]
        xn = x_ref[:, sl] + g_ref[:, sl] * y
        if final:
            ssq = ssq + jnp.sum(xn * xn, axis=-1, keepdims=True)
        o_ref[:, sl] = xn
    if final:
        o_ref[...] = o_ref[...] * lax.rsqrt(ssq / (ns * LANES) + EPS) * ng_ref[...]


def _combine(ys, route, x_all, mod3, layer, n_rows, dims, tc, norm_f_g=None):
    d = x_all.shape[1]
    ns = d // LANES
    n_lat = dims["T"] // tc
    mm = functools.partial(_mod_row_map, layer, tc, n_lat, dims["S"], dims["B"])
    final = norm_f_g is not None
    in_specs = [
        pl.BlockSpec((tc * 2 * ns, LANES), lambda i: (i, 0)),
        pl.BlockSpec((tc, d), lambda i: (i, 0)),
        pl.BlockSpec((None, 1, d), mm(5)),
        pl.BlockSpec((tc, LANES), lambda i: (i, 0)),
    ]
    args = [ys, x_all, mod3, route]
    if final:
        in_specs.append(pl.BlockSpec((1, d), lambda i: (0, 0)))
        args.append(norm_f_g.reshape(1, d))
    return pl.pallas_call(
        functools.partial(_combine_kernel, tc=tc, ns=ns, final=final),
        out_shape=jax.ShapeDtypeStruct((n_rows, d), F32),
        grid=(n_rows // tc,),
        in_specs=in_specs,
        out_specs=pl.BlockSpec((tc, d), lambda i: (i, 0)),
        compiler_params=_params("parallel"),
        name="moe_combine_residual",
    )(*args)


def _rope_table(s):
    half = ATT_DQK // 2
    rows = jnp.repeat(jnp.arange(s // GRID_W, dtype=jnp.int32), GRID_W)
    cols = jnp.tile(jnp.arange(GRID_W, dtype=jnp.int32), s // GRID_W)
    inv = ROPE_BASE ** (-jnp.arange(0, half, 2, dtype=F32) / half)
    ar = rows.astype(F32)[:, None] * inv
    ac = cols.astype(F32)[:, None] * inv
    cr, sr, cc, sc = jnp.cos(ar), jnp.sin(ar), jnp.cos(ac), jnp.sin(ac)
    z = jnp.zeros_like(sr)
    c = jnp.concatenate([cr, cr, cc, cc] * 2, axis=-1)
    s1 = jnp.concatenate([-sr, z, -sc, z] * 2, axis=-1)
    s2 = jnp.concatenate([z, sr, z, sc] * 2, axis=-1)
    return jnp.concatenate([c, s1, s2], axis=-1)


def kernel(x, c, ctx, c_ctx, w_mod, b_mod, norm1_g, norm2_g, w_in, lam_q1, lam_k1, lam_q2, lam_k2, attn_subln_g, w_attn_o, conf_dw_w, conf_dw_b, conf_ln_g, conf_ln_b, w_conf_o, hy_sc_w, hy_sc_b, hy_w1, hy_b1, hy_w2, hy_b2, hy_freq, hy_w3, hy_decay, hy_bias, w_hy_o, w_out, w_router_group, w_router_expert, w_exp_gate, w_exp_up, w_exp_down, norm_f_g):
    b, s, d = x.shape
    lc = ctx.shape[1]
    depth = w_mod.shape[0]
    t, tc_rows = b * s, b * lc
    cw, hw = conf_dw_w.shape[2], w_hy_o.shape[1]
    dims = dict(B=b, S=s, Lc=lc, T=t, QK=ATT_HEADS * 2 * ATT_DQK, AW=ATT_HEADS * ATT_DV, CW=cw, HW=hw)
    assert b + 1 <= MOD_ROWS and 2 * ATT_DQK == LANES and ATT_DV == LANES and hw == cw
    tm = _largest_tile((1024, 512, 256, 128), s, tc_rows)
    ts = _largest_tile((256, 128), s, lc)
    tn = _largest_tile((512, 256, 128), dims["QK"], dims["AW"], 2 * cw, 3 * hw, d)
    tcomb = _largest_tile((256, 128), s, tc_rows)

    x_all = jnp.concatenate([x.reshape(t, d), ctx.reshape(tc_rows, d)], axis=0)
    cond = jnp.zeros((MOD_ROWS, d), F32).at[:b].set(c).at[b].set(c_ctx)
    mod3 = _modulation(cond, w_mod, b_mod).reshape(depth * MOD_ROWS, 1, 6 * d)
    rope_tab = _rope_table(s)
    tables_lat, emb_lat = _dft_tables(s), _hyena_embedding(s)
    tables_ctx, emb_ctx = _dft_tables(lc), _hyena_embedding(lc)
    hy_params = dict(hy_w1=hy_w1, hy_b1=hy_b1, hy_w2=hy_w2, hy_b2=hy_b2, hy_freq=hy_freq, hy_w3=hy_w3,
                     hy_decay=hy_decay, hy_bias=hy_bias)

    out = None
    for l in range(depth):
        need_ctx = l < depth - 1
        n_rows = t + tc_rows if need_ctx else t
        lam_init = 0.8 - 0.6 * math.exp(-0.3 * l)
        lam = (jnp.exp(jnp.sum(lam_q1[l] * lam_k1[l])) - jnp.exp(jnp.sum(lam_q2[l] * lam_k2[l])) + lam_init)
        lam = lam.reshape(1).astype(F32)

        hl = _adanorm(x_all, norm1_g, mod3, l, dims, tm)
        qkv, conv_in, gates = _in_proj(hl, w_in, l, rope_tab, dims, tm, tn)
        att, att_ctx = _attention(qkv, lam, attn_subln_g, l, lam_init, dims, need_ctx)
        conf = _conformer(conv_in, conf_dw_w, conf_dw_b, conf_ln_g, conf_ln_b, l, dims, n_rows, ts)
        u = _hyena_short(conv_in, hy_sc_w, hy_sc_b, l, dims, n_rows, ts)
        hy = _hyena_long(u, 0, b, s, l, hy_params, tables_lat, emb_lat, dims)
        hy_ctx = _hyena_long(u, t, b, lc, l, hy_params, tables_ctx, emb_ctx, dims) if need_ctx else hy
        merged = _merge(att, att_ctx, conf, hy, hy_ctx, gates, w_attn_o, w_conf_o, w_hy_o, l, n_rows, t, d, tm, tn)
        x_all = _out_proj(merged, w_out, x_all, mod3, l, n_rows, dims, tm, tn)

        w_router = jnp.concatenate(
            [w_router_group[l], jnp.transpose(w_router_expert[l], (1, 0, 2)).reshape(d, N_EXPERTS)], axis=1)
        w_router = jnp.pad(w_router, ((0, 0), (0, LANES - w_router.shape[1])))
        h2t, route = _router(x_all, norm2_g, mod3, w_router, l, n_rows, dims, tm)
        order, blk_e, blk_src0, blk_nvalid, n_used, n_blocks = _dispatch(route, n_rows, MOE_BLOCK)
        xs = _gather_rows(h2t, order, blk_src0, n_used, n_blocks, MOE_BLOCK, d)
        ys = _experts(xs, order, blk_e, blk_src0, blk_nvalid, n_used, w_exp_gate, w_exp_up, w_exp_down, l,
                      n_blocks, MOE_BLOCK)
        x_new = _combine(ys, route, x_all, mod3, l, n_rows, dims, tcomb, None if need_ctx else norm_f_g)
        if need_ctx:
            x_all = x_new
        else:
            out = x_new
    return out.reshape(b, s, d)
```

```python
import functools
import math

import jax
import jax.numpy as jnp
from jax import lax
from jax.experimental import pallas as pl
from jax.experimental.pallas import tpu as pltpu

F32 = jnp.float32
BF16 = jnp.bfloat16

GRID_W = 64
EPS = 1e-6
ATT_HEADS = 8
ATT_DQK = 64
ATT_DV = 2 * ATT_DQK
ROPE_BASE = 10000.0
CONF_K = 31
HY_ORDER = 2
HY_SHORT_K = 3
HY_EMB = 33
HY_BANDS = (HY_EMB - 1) // 2
N_BRANCH = 3
N_GROUPS = 8
E_PER_GROUP = 8
N_EXPERTS = N_GROUPS * E_PER_GROUP

LANES = 128
SUBLANES = 8
VMEM_LIMIT_BYTES = 56 * 1024 * 1024

CONF_HALO = 16
HY_HALO = 8
MOE_BLOCK = 256
ATT_UNROLL = 16
ATT_TQ = (2048, 1024, 512, 256, 128)
ATT_TK = (256, 128)
MOD_ROWS = 8


def _params(*sem):
    return pltpu.CompilerParams(dimension_semantics=sem, vmem_limit_bytes=VMEM_LIMIT_BYTES)


def _largest_tile(cands, *dims):
    for c in cands:
        if all(d % c == 0 for d in dims):
            return c
    raise ValueError(f"no tile in {cands} divides {dims}")


def _dot(a, b):
    return jnp.dot(a, b, preferred_element_type=F32)


def _split_bf16(x):
    hi = x.astype(BF16)
    lo = (x - hi.astype(F32)).astype(BF16)
    return hi, lo


def _dot3(a, b):
    ah, al = _split_bf16(a)
    bh, bl = _split_bf16(b)
    return _dot(ah, bh) + _dot(al, bh) + _dot(ah, bl)


def _mod_kernel(c_ref, w_ref, b_ref, o_ref):
    c = c_ref[...]
    s = c * jax.nn.sigmoid(c)
    o_ref[...] = _dot3(s, w_ref[...]) + b_ref[...]


def _modulation(cond, w_mod, b_mod):
    depth, d, n = w_mod.shape
    tn = _largest_tile((512, 256, 128), n)
    return pl.pallas_call(
        _mod_kernel,
        out_shape=jax.ShapeDtypeStruct((depth, MOD_ROWS, n), F32),
        grid=(depth, n // tn),
        in_specs=[
            pl.BlockSpec((MOD_ROWS, d), lambda l, j: (0, 0)),
            pl.BlockSpec((None, d, tn), lambda l, j: (l, 0, j)),
            pl.BlockSpec((None, 1, tn), lambda l, j: (l, 0, j)),
        ],
        out_specs=pl.BlockSpec((None, MOD_ROWS, tn), lambda l, j: (l, 0, j)),
        compiler_params=_params("parallel", "parallel"),
        name="modulation",
    )(cond, w_mod, b_mod.reshape(depth, 1, n))


def _ada_norm_value(x, g, shift, scale):
    ms = jnp.mean(x * x, axis=-1, keepdims=True)
    y = x * lax.rsqrt(ms + EPS) * g
    return y * (1.0 + scale) + shift


def _adanorm_kernel(x_ref, g_ref, sh_ref, sc_ref, o_ref):
    o_ref[...] = _ada_norm_value(x_ref[...], g_ref[...], sh_ref[...], sc_ref[...]).astype(o_ref.dtype)


def _mod_row_map(layer, tm, n_lat, s, b, col):
    def index_map(i, *_):
        row = jnp.where(i < n_lat, (i * tm) // s, b)
        return (layer * MOD_ROWS + row, 0, col)
    return index_map


def _adanorm(x_all, norm_g, mod3, layer, dims, tm):
    r, d = x_all.shape
    n_lat = dims["T"] // tm
    mm = functools.partial(_mod_row_map, layer, tm, n_lat, dims["S"], dims["B"])
    return pl.pallas_call(
        _adanorm_kernel,
        out_shape=jax.ShapeDtypeStruct((r, d), BF16),
        grid=(r // tm,),
        in_specs=[
            pl.BlockSpec((tm, d), lambda i: (i, 0)),
            pl.BlockSpec((None, 1, d), lambda i: (layer, 0, 0)),
            pl.BlockSpec((None, 1, d), mm(0)),
            pl.BlockSpec((None, 1, d), mm(1)),
        ],
        out_specs=pl.BlockSpec((tm, d), lambda i: (i, 0)),
        compiler_params=_params("parallel"),
        name="adanorm1",
    )(x_all, norm_g.reshape(norm_g.shape[0], 1, d), mod3, mod3)


def _qkv_kernel(a_ref, w_ref, tab_ref, o_ref, *, n_lat, tn, qk_cols):
    i = pl.program_id(0)
    j = pl.program_id(1)
    acc = _dot(a_ref[...], w_ref[...].astype(BF16))
    col0 = j * tn
    scale = jnp.where(col0 < qk_cols, ATT_DQK ** -0.5 * math.log2(math.e), 1.0).astype(F32)
    do_rope = jnp.logical_and(i < n_lat, col0 < 2 * qk_cols)

    @pl.when(do_rope)
    def _():
        c = tab_ref[:, 0:LANES]
        s1 = tab_ref[:, LANES:2 * LANES]
        s2 = tab_ref[:, 2 * LANES:3 * LANES]
        for h in range(tn // LANES):
            xh = acc[:, h * LANES:(h + 1) * LANES] * scale
            rot = xh * c + pltpu.roll(xh, LANES - 16, 1) * s1 + pltpu.roll(xh, 16, 1) * s2
            o_ref[:, h * LANES:(h + 1) * LANES] = rot.astype(o_ref.dtype)

    @pl.when(jnp.logical_not(do_rope))
    def _():
        o_ref[...] = (acc * scale).astype(o_ref.dtype)


def _plain_mm_kernel(a_ref, w_ref, o_ref):
    o_ref[...] = _dot(a_ref[...], w_ref[...].astype(BF16)).astype(o_ref.dtype)


def _sigmoid_mm_kernel(a_ref, w_ref, o_ref):
    o_ref[...] = jax.nn.sigmoid(_dot(a_ref[...], w_ref[...].astype(BF16))).astype(o_ref.dtype)


def _in_proj(hl, w_in, layer, rope_tab, dims, tm, tn):
    r, d = hl.shape
    qk, aw, cw, hw = dims["QK"], dims["AW"], dims["CW"], dims["HW"]
    n_lat = dims["T"] // tm
    s_tiles = dims["S"] // tm
    a_spec = pl.BlockSpec((tm, d), lambda i, j: (i, 0))

    def w_spec(col_off):
        off = col_off // tn
        return pl.BlockSpec((None, d, tn), lambda i, j: (layer, 0, j + off))

    def call(kernel, col_off, n_cols, dtype, name, extra_in=(), extra_specs=()):
        return pl.pallas_call(
            kernel,
            out_shape=jax.ShapeDtypeStruct((r, n_cols), dtype),
            grid=(r // tm, n_cols // tn),
            in_specs=[a_spec, w_spec(col_off), *extra_specs],
            out_specs=pl.BlockSpec((tm, tn), lambda i, j: (i, j)),
            compiler_params=_params("parallel", "arbitrary"),
            name=name,
        )(hl, w_in, *extra_in)

    qkv = call(
        functools.partial(_qkv_kernel, n_lat=n_lat, tn=tn, qk_cols=qk), 0, 2 * qk + aw, BF16, "in_proj_qkv",
        extra_in=(rope_tab,),
        extra_specs=(pl.BlockSpec((tm, 3 * LANES), lambda i, j: (i % s_tiles, 0)),))
    conv_in = call(_plain_mm_kernel, 2 * qk + aw, 2 * cw + 3 * hw, F32, "in_proj_conv")
    gates = call(_sigmoid_mm_kernel, 2 * qk + aw + 2 * cw + 3 * hw, N_BRANCH * d, BF16, "in_proj_gates")
    return qkv, conv_in, gates


def _attn_kernel(lam_ref, q_ref, g_ref, kc_ref, vc_ref, *rest, n_lat_chunks, tk, out_scale, unroll):
    if n_lat_chunks:
        kl_ref, vl_ref, o_ref, vce, vle = rest
    else:
        o_ref, vce = rest

    @pl.when(pl.program_id(2) == 0)
    def _():
        vce[:, 0:ATT_DV] = vc_ref[...]
        vce[:, ATT_DV:] = jnp.ones((vce.shape[0], ATT_DV), BF16)
        if n_lat_chunks:
            vle[:, 0:ATT_DV] = vl_ref[...]
            vle[:, ATT_DV:] = jnp.ones((vle.shape[0], ATT_DV), BF16)

    q = q_ref[...]
    lane = lax.broadcasted_iota(jnp.int32, q.shape, 1)
    zero = jnp.zeros_like(q)
    q1 = jnp.where(lane < ATT_DQK, q, zero)
    q2 = jnp.where(lane >= ATT_DQK, q, zero)
    tq = q.shape[0]
    nt = (((1,), (1,)), ((), ()))

    def one_map(qm, k, v, m, acc):
        s = lax.dot_general(qm, k, nt, preferred_element_type=F32)
        m_new = jnp.maximum(m, jnp.max(s, axis=-1, keepdims=True))
        alpha = jnp.exp2(m - m_new)
        p = jnp.exp2((s - m_new).astype(BF16))
        return m_new, alpha * acc + _dot(p, v)

    def step(k, v, carry):
        m1, a1, m2, a2 = carry
        m1, a1 = one_map(q1, k, v, m1, a1)
        m2, a2 = one_map(q2, k, v, m2, a2)
        return m1, a1, m2, a2

    neg = jnp.full((tq, 1), -jnp.inf, F32)
    za = jnp.zeros((tq, 2 * ATT_DV), F32)
    carry = step(kc_ref[...], vce[...], (neg, za, neg, za))
    if n_lat_chunks:
        def body(c, carry):
            off = pl.multiple_of(c * tk, tk)
            return step(kl_ref[pl.ds(off, tk), :], vle[pl.ds(off, tk), :], carry)
        carry = lax.fori_loop(0, n_lat_chunks, body, carry, unroll=unroll)
    _, a1, _, a2 = carry
    o = a1[:, 0:ATT_DV] / a1[:, ATT_DV:] - lam_ref[0] * (a2[:, 0:ATT_DV] / a2[:, ATT_DV:])
    ms = jnp.mean(o * o, axis=-1, keepdims=True)
    o = o * lax.rsqrt(ms + 1e-5) * g_ref[...] * out_scale
    o_ref[...] = o.astype(o_ref.dtype)


def _attention(qkv, lam, subln_g, layer, lam_init, dims, with_ctx):
    b, s, lc, t = dims["B"], dims["S"], dims["Lc"], dims["T"]
    h = ATT_HEADS
    tq = _largest_tile(ATT_TQ, s)
    tk = _largest_tile(ATT_TK, s)
    tqc = _largest_tile((512, 256, 128), lc)
    smem = pl.BlockSpec(memory_space=pltpu.SMEM)
    g3 = subln_g.reshape(subln_g.shape[0], 1, ATT_DV)
    g_spec = pl.BlockSpec((None, 1, ATT_DV), lambda bi, hi, ti: (layer, 0, 0))
    kc_spec = pl.BlockSpec((lc, LANES), lambda bi, hi, ti: (t // lc + bi, h + hi))
    vc_spec = pl.BlockSpec((lc, LANES), lambda bi, hi, ti: (t // lc + bi, 2 * h + hi))
    out_scale = 1.0 - lam_init

    sem = ("arbitrary", "arbitrary", "arbitrary")
    att = pl.pallas_call(
        functools.partial(_attn_kernel, n_lat_chunks=s // tk, tk=tk, out_scale=out_scale,
                          unroll=min(ATT_UNROLL, s // tk)),
        out_shape=jax.ShapeDtypeStruct((t, h * ATT_DV), BF16),
        grid=(b, h, s // tq),
        in_specs=[
            smem,
            pl.BlockSpec((tq, LANES), lambda bi, hi, ti: (bi * (s // tq) + ti, hi)),
            g_spec, kc_spec, vc_spec,
            pl.BlockSpec((s, LANES), lambda bi, hi, ti: (bi, h + hi)),
            pl.BlockSpec((s, LANES), lambda bi, hi, ti: (bi, 2 * h + hi)),
        ],
        out_specs=pl.BlockSpec((tq, LANES), lambda bi, hi, ti: (bi * (s // tq) + ti, hi)),
        scratch_shapes=[pltpu.VMEM((lc, 2 * ATT_DV), BF16), pltpu.VMEM((s, 2 * ATT_DV), BF16)],
        compiler_params=_params(*sem),
        name="diff_attn_latent",
    )(lam, qkv, g3, qkv, qkv, qkv, qkv)
    if not with_ctx:
        return att, att

    att_ctx = pl.pallas_call(
        functools.partial(_attn_kernel, n_lat_chunks=0, tk=tk, out_scale=out_scale, unroll=1),
        out_shape=jax.ShapeDtypeStruct((b * lc, h * ATT_DV), BF16),
        grid=(b, h, lc // tqc),
        in_specs=[
            smem,
            pl.BlockSpec((tqc, LANES), lambda bi, hi, ti: ((t + bi * lc) // tqc + ti, hi)),
            g_spec, kc_spec, vc_spec,
        ],
        out_specs=pl.BlockSpec((tqc, LANES), lambda bi, hi, ti: (bi * (lc // tqc) + ti, hi)),
        scratch_shapes=[pltpu.VMEM((lc, 2 * ATT_DV), BF16)],
        compiler_params=_params(*sem),
        name="diff_attn_ctx",
    )(lam, qkv, g3, qkv, qkv)
    return att, att_ctx


def _seq_edges(i, n_lat, tps_lat, tps_ctx):
    is_lat = i < n_lat
    pos = jnp.where(is_lat, i % tps_lat, (i - n_lat) % tps_ctx)
    n = jnp.where(is_lat, tps_lat, tps_ctx)
    return pos == 0, pos == n - 1


def _halo_specs(ts, halo, width, col, n_rows):
    per = ts // halo
    last = n_rows // halo - 1
    prev = pl.BlockSpec((halo, width), lambda i, *_: (jnp.maximum(i * per - 1, 0), col(*_)))
    cur = pl.BlockSpec((ts, width), lambda i, *_: (i, col(*_)))
    nxt = pl.BlockSpec((halo, width), lambda i, *_: (jnp.minimum((i + 1) * per, last), col(*_)))
    return prev, cur, nxt


def _conf_kernel(ap_ref, ac_ref, an_ref, gp_ref, gc_ref, gn_ref, w_ref, b_ref, lg_ref, lb_ref, o_ref, ybuf, ysh,
                 *, ts, n_lat, tps_lat, tps_ctx, row_chunk):
    first, last = _seq_edges(pl.program_id(0), n_lat, tps_lat, tps_ctx)
    halo = CONF_HALO

    def glu(a, g):
        return a * jax.nn.sigmoid(g)

    prev = glu(ap_ref[...], gp_ref[...])
    nxt = glu(an_ref[...], gn_ref[...])
    ybuf[0:halo, :] = jnp.where(first, jnp.zeros_like(prev), prev)
    ybuf[halo:halo + ts, :] = glu(ac_ref[...], gc_ref[...])
    ybuf[halo + ts:, :] = jnp.where(last, jnp.zeros_like(nxt), nxt)
    n_sh = ysh.shape[1]
    for b in range(1, SUBLANES):
        ysh[b] = ybuf[b:b + n_sh, :]
    base = halo - CONF_K // 2
    for r0 in range(0, ts, row_chunk):
        acc = jnp.zeros((row_chunk, ybuf.shape[1]), F32) + b_ref[...]
        for j in range(CONF_K):
            b = (base + j) % SUBLANES
            start = r0 + base + j - b
            rows = ybuf[start:start + row_chunk, :] if b == 0 else ysh[b, start:start + row_chunk, :]
            acc = acc + w_ref[j:j + 1, :] * rows
        mu = jnp.mean(acc, axis=-1, keepdims=True)
        xc = acc - mu
        var = jnp.mean(xc * xc, axis=-1, keepdims=True)
        y = xc * lax.rsqrt(var + 1e-5) * lg_ref[...] + lb_ref[...]
        o_ref[r0:r0 + row_chunk, :] = (y * jax.nn.sigmoid(y)).astype(o_ref.dtype)


def _conformer(conv_in, dw_w, dw_b, ln_g, ln_b, layer, dims, n_rows, ts):
    cw = dims["CW"]
    depth = dw_w.shape[0]
    n_lat = dims["T"] // ts
    a_specs = _halo_specs(ts, CONF_HALO, cw, lambda: 0, conv_in.shape[0])
    g_specs = _halo_specs(ts, CONF_HALO, cw, lambda: 1, conv_in.shape[0])
    vec = lambda: pl.BlockSpec((None, 1, cw), lambda i: (layer, 0, 0))
    kern = functools.partial(_conf_kernel, ts=ts, n_lat=n_lat, tps_lat=dims["S"] // ts, tps_ctx=dims["Lc"] // ts,
                             row_chunk=min(32, ts))
    return pl.pallas_call(
        kern,
        out_shape=jax.ShapeDtypeStruct((n_rows, cw), BF16),
        grid=(n_rows // ts,),
        in_specs=[*a_specs, *g_specs,
                  pl.BlockSpec((None, CONF_K, cw), lambda i: (layer, 0, 0)), vec(), vec(), vec()],
        out_specs=pl.BlockSpec((ts, cw), lambda i: (i, 0)),
        scratch_shapes=[pltpu.VMEM((ts + 2 * CONF_HALO, cw), F32),
                        pltpu.VMEM((SUBLANES, ts + 2 * CONF_HALO - SUBLANES, cw), F32)],
        compiler_params=_params("parallel"),
        name="conformer_conv",
    )(conv_in, conv_in, conv_in, conv_in, conv_in, conv_in, dw_w,
      dw_b.reshape(depth, 1, cw), ln_g.reshape(depth, 1, cw), ln_b.reshape(depth, 1, cw))


def _hy_short_kernel(p_ref, c_ref, n_ref, w_ref, b_ref, o_ref, ybuf, *, ts, n_lat, tps_lat, tps_ctx):
    first, last = _seq_edges(pl.program_id(0), n_lat, tps_lat, tps_ctx)
    halo = HY_HALO
    prev = p_ref[...]
    nxt = n_ref[...]
    ybuf[0:halo, :] = jnp.where(first, jnp.zeros_like(prev), prev)
    ybuf[halo:halo + ts, :] = c_ref[...]
    ybuf[halo + ts:, :] = jnp.where(last, jnp.zeros_like(nxt), nxt)
    acc = b_ref[...] + w_ref[0:1, :] * ybuf[halo - 1:halo - 1 + ts, :]
    acc = acc + w_ref[1:2, :] * ybuf[halo:halo + ts, :]
    acc = acc + w_ref[2:3, :] * ybuf[halo + 1:halo + 1 + ts, :]
    o_ref[...] = acc


def _hyena_short(conv_in, sc_w, sc_b, layer, dims, n_rows, ts):
    cw, hw = dims["CW"], dims["HW"]
    depth = sc_w.shape[0]
    n_lat = dims["T"] // ts
    col0 = 2 * cw // hw
    specs = _halo_specs(ts, HY_HALO, hw, lambda m: col0 + m, conv_in.shape[0])
    kern = functools.partial(_hy_short_kernel, ts=ts, n_lat=n_lat, tps_lat=dims["S"] // ts,
                             tps_ctx=dims["Lc"] // ts)
    return pl.pallas_call(
        kern,
        out_shape=jax.ShapeDtypeStruct((n_rows, 3 * hw), F32),
        grid=(n_rows // ts, 3),
        in_specs=[*specs,
                  pl.BlockSpec((None, HY_SHORT_K, hw), lambda i, m: (layer, 0, m)),
                  pl.BlockSpec((None, 1, hw), lambda i, m: (layer, 0, m))],
        out_specs=pl.BlockSpec((ts, hw), lambda i, m: (i, m)),
        scratch_shapes=[pltpu.VMEM((ts + 2 * HY_HALO, hw), F32)],
        compiler_params=_params("parallel", "arbitrary"),
        name="hyena_short_conv",
    )(conv_in, conv_in, conv_in, sc_w, sc_b.reshape(depth, 1, 3 * hw))


def _filter_kernel(emb_ref, w1_ref, b1_ref, f_ref, w2_ref, b2_ref, w3_ref, dec_ref, h_ref, asum_ref, *, hw, tl):
    i = pl.program_id(0)
    hp = lax.Precision.HIGHEST
    emb = emb_ref[...]
    t = emb[:, 0:1]
    h1 = jnp.sin(f_ref[0:1, :] * (jnp.dot(emb, w1_ref[...], precision=hp, preferred_element_type=F32) + b1_ref[...]))
    h2 = jnp.sin(f_ref[1:2, :] * (jnp.dot(h1, w2_ref[...], precision=hp, preferred_element_type=F32) + b2_ref[...]))
    h = jnp.dot(h2, w3_ref[...], precision=hp, preferred_element_type=F32)
    h = h * jnp.exp(-t * jnp.abs(dec_ref[...]))
    col = lax.broadcasted_iota(jnp.int32, h.shape, 1)
    row = lax.broadcasted_iota(jnp.int32, h.shape, 0) + i * tl
    backward = (col // hw) % 2 == 1
    h = jnp.where(jnp.logical_and(backward, row == 0), 0.0, h)
    h_ref[...] = h.astype(h_ref.dtype)

    @pl.when(i == 0)
    def _():
        asum_ref[...] = jnp.zeros_like(asum_ref)

    part = jnp.abs(h).reshape(tl // SUBLANES, SUBLANES, h.shape[1]).sum(axis=0)
    asum_ref[...] += part


def _hyena_filter(emb, w1, b1, freq, w2, b2, w3, decay, layer, hw):
    l, ke = emb.shape
    depth, _, ffn = w1.shape
    nf = w3.shape[2]
    tl = _largest_tile((512, 256, 128), l)
    w1p = jnp.pad(w1, ((0, 0), (0, ke - w1.shape[1]), (0, 0)))
    vec = lambda n: pl.BlockSpec((None, 1, n), lambda i: (layer, 0, 0))
    return pl.pallas_call(
        functools.partial(_filter_kernel, hw=hw, tl=tl),
        out_shape=(jax.ShapeDtypeStruct((l, nf), BF16), jax.ShapeDtypeStruct((SUBLANES, nf), F32)),
        grid=(l // tl,),
        in_specs=[
            pl.BlockSpec((tl, ke), lambda i: (i, 0)),
            pl.BlockSpec((None, ke, ffn), lambda i: (layer, 0, 0)),
            vec(ffn),
            pl.BlockSpec((None, 2, ffn), lambda i: (layer, 0, 0)),
            pl.BlockSpec((None, ffn, ffn), lambda i: (layer, 0, 0)),
            vec(ffn),
            pl.BlockSpec((None, ffn, nf), lambda i: (layer, 0, 0)),
            vec(nf),
        ],
        out_specs=(pl.BlockSpec((tl, nf), lambda i: (i, 0)), pl.BlockSpec((SUBLANES, nf), lambda i: (0, 0))),
        compiler_params=_params("arbitrary"),
        name="hyena_filter_mlp",
    )(emb, w1p, b1.reshape(depth, 1, ffn), freq, w2, b2.reshape(depth, 1, ffn), w3, decay.reshape(depth, 1, nf))


def _bin_scale(i, tm, n_bins):
    row = lax.broadcasted_iota(jnp.int32, (tm, 1), 0) + i * tm
    is0 = row == 0
    return is0, jnp.where(is0, 1.0 / (2 * n_bins), 2.0 / (2 * n_bins)).astype(F32)


def _filter_dft_kernel(c_ref, s_ref, hf_ref, hb_ref, af_ref, ab_ref, kre_ref, kim_ref, *, tm, n_bins):
    i = pl.program_id(1)
    c = c_ref[...]
    s = s_ref[...]
    fre, fim = _dot(c, hf_ref[...]), _dot(s, hf_ref[...])
    bre, bim = _dot(c, hb_ref[...]), _dot(s, hb_ref[...])
    norm = jnp.sum(af_ref[...], axis=0, keepdims=True) + jnp.sum(ab_ref[...], axis=0, keepdims=True)
    is0, cs = _bin_scale(i, tm, n_bins)
    w = cs / norm
    kre_ref[...] = (fre + bre) * w
    kim_ref[...] = jnp.where(is0, fim + bim, fim - bim) * w


def _filter_spectrum(h, asum, c_tab, s_tab, hw):
    l, nf = h.shape
    n_ord = nf // (2 * hw)
    tm = _largest_tile((512, 256, 128), l)
    out = jax.ShapeDtypeStruct((l, n_ord * hw), F32)
    tab = pl.BlockSpec((tm, l), lambda o, i: (i, 0))
    return pl.pallas_call(
        functools.partial(_filter_dft_kernel, tm=tm, n_bins=l),
        out_shape=(out, out),
        grid=(n_ord, l // tm),
        in_specs=[tab, tab,
                  pl.BlockSpec((l, hw), lambda o, i: (0, 2 * o)),
                  pl.BlockSpec((l, hw), lambda o, i: (0, 2 * o + 1)),
                  pl.BlockSpec((SUBLANES, hw), lambda o, i: (0, 2 * o)),
                  pl.BlockSpec((SUBLANES, hw), lambda o, i: (0, 2 * o + 1))],
        out_specs=(pl.BlockSpec((tm, hw), lambda o, i: (i, o)), pl.BlockSpec((tm, hw), lambda o, i: (i, o))),
        compiler_params=_params("parallel", "arbitrary"),
        name="hyena_filter_dft",
    )(c_tab, s_tab, h, h, asum, asum)


def _fwd_dft_kernel(c_ref, s_ref, z_ref, kre_ref, kim_ref, yre_ref, yim_ref, zb, *, tm):
    i = pl.program_id(1)

    @pl.when(i == 0)
    def _():
        zb[...] = z_ref[...].astype(BF16)

    xre = _dot(c_ref[...], zb[...])
    xim = _dot(s_ref[...], zb[...])
    kre, kim = kre_ref[...], kim_ref[...]
    is0 = (lax.broadcasted_iota(jnp.int32, (tm, 1), 0) + i * tm) == 0
    yre = xre * kre - jnp.where(is0, 0.0, xim * kim)
    yim = jnp.where(is0, xim * kim, xre * kim + xim * kre)
    yre_ref[...] = yre.astype(BF16)
    yim_ref[...] = yim.astype(BF16)


def _fwd_dft(z_arr, z_row0, z_col, kre, kim, order, c_tab, s_tab, b, l, hw):
    tm = _largest_tile((512, 256, 128), l)
    out = jax.ShapeDtypeStruct((b * l, hw), BF16)
    tab = pl.BlockSpec((tm, l), lambda bi, i: (i, 0))
    kspec = pl.BlockSpec((tm, hw), lambda bi, i: (i, order))
    ospec = pl.BlockSpec((tm, hw), lambda bi, i: (bi * (l // tm) + i, 0))
    return pl.pallas_call(
        functools.partial(_fwd_dft_kernel, tm=tm),
        out_shape=(out, out),
        grid=(b, l // tm),
        in_specs=[tab, tab, pl.BlockSpec((l, hw), lambda bi, i: (z_row0 // l + bi, z_col)), kspec, kspec],
        out_specs=(ospec, ospec),
        scratch_shapes=[pltpu.VMEM((l, hw), BF16)],
        compiler_params=_params("parallel", "arbitrary"),
        name="hyena_fwd_dft",
    )(c_tab, s_tab, z_arr, kre, kim)


def _inv_dft_kernel(c_ref, st_ref, yre_ref, yim_ref, gate_ref, z_ref, bias_ref, o_ref):
    y = _dot(c_ref[...], yre_ref[...]) + _dot(st_ref[...], yim_ref[...])
    o_ref[...] = (gate_ref[...] * (y + bias_ref[...] * z_ref[...])).astype(o_ref.dtype)


def _inv_dft(yre, yim, u, u_row0, gate_col, z_arr, z_row0, z_col, bias, layer, order, c_tab, st_tab, b, l, hw,
             out_dtype):
    tm = _largest_tile((512, 256, 128), l)
    per = l // tm
    tab = pl.BlockSpec((tm, l), lambda bi, i: (i, 0))
    yspec = pl.BlockSpec((l, hw), lambda bi, i: (bi, 0))
    return pl.pallas_call(
        _inv_dft_kernel,
        out_shape=jax.ShapeDtypeStruct((b * l, hw), out_dtype),
        grid=(b, per),
        in_specs=[tab, tab, yspec, yspec,
                  pl.BlockSpec((tm, hw), lambda bi, i: (u_row0 // tm + bi * per + i, gate_col)),
                  pl.BlockSpec((tm, hw), lambda bi, i: (z_row0 // tm + bi * per + i, z_col)),
                  pl.BlockSpec((None, None, 1, hw), lambda bi, i: (layer, order, 0, 0))],
        out_specs=pl.BlockSpec((tm, hw), lambda bi, i: (bi * per + i, 0)),
        compiler_params=_params("parallel", "arbitrary"),
        name="hyena_inv_dft",
    )(c_tab, st_tab, yre, yim, u, z_arr, bias.reshape(bias.shape[0], bias.shape[1], 1, hw))


def _dft_table_kernel(c_ref, s_ref, st_ref, *, tm, l):
    n = 2 * l
    w = 2.0 * math.pi / n
    row = lax.broadcasted_iota(jnp.int32, (tm, LANES), 0) + pl.program_id(0) * tm
    lane = lax.broadcasted_iota(jnp.int32, (tm, LANES), 1)
    fine = ((row * lane) & (n - 1)).astype(F32) * w
    coarse = ((row * (lane * LANES)) & (n - 1)).astype(F32) * w
    cb, sb = jnp.cos(fine), jnp.sin(fine)
    ca, sa = jnp.cos(coarse), jnp.sin(coarse)
    alt_row = (1 - 2 * (row & 1)).astype(F32)
    for j in range(l // LANES):
        caj = jnp.broadcast_to(ca[:, j:j + 1], (tm, LANES))
        saj = jnp.broadcast_to(sa[:, j:j + 1], (tm, LANES))
        col = lane + j * LANES
        cos_blk = caj * cb - saj * sb
        nsin_blk = -(saj * cb + caj * sb)
        alt_col = (1 - 2 * (col & 1)).astype(F32)
        sl = slice(j * LANES, (j + 1) * LANES)
        c_ref[:, sl] = cos_blk.astype(BF16)
        s_ref[:, sl] = jnp.where(row == 0, alt_col, nsin_blk).astype(BF16)
        st_ref[:, sl] = jnp.where(col == 0, alt_row, nsin_blk).astype(BF16)


def _dft_tables(l):
    assert l % LANES == 0 and (l & (l - 1)) == 0 and l // LANES <= LANES
    tm = _largest_tile((256, 128), l)
    out = jax.ShapeDtypeStruct((l, l), BF16)
    spec = pl.BlockSpec((tm, l), lambda i: (i, 0))
    return pl.pallas_call(
        functools.partial(_dft_table_kernel, tm=tm, l=l),
        out_shape=(out, out, out),
        grid=(l // tm,),
        out_specs=(spec, spec, spec),
        compiler_params=_params("parallel"),
        name="dft_tables",
    )()


def _hyena_embedding(l):
    n = jnp.arange(l, dtype=F32)
    t = n / max(l - 1, 1)
    w = 2.0 * math.pi * n / l
    f = jnp.linspace(1e-4, HY_BANDS - 1, HY_BANDS, dtype=F32)
    fw = w[:, None] * f[None, :]
    emb = jnp.concatenate([t[:, None], jnp.cos(fw), -jnp.sin(fw)], axis=-1)
    return jnp.pad(emb, ((0, 0), (0, LANES - HY_EMB)))


def _hyena_long(u, u_row0, b, l, layer, p, tables, emb, dims):
    hw = dims["HW"]
    c_tab, s_tab, st_tab = tables
    h, asum = _hyena_filter(emb, p["hy_w1"], p["hy_b1"], p["hy_freq"], p["hy_w2"], p["hy_b2"], p["hy_w3"],
                            p["hy_decay"], layer, hw)
    kre, kim = _filter_spectrum(h, asum, c_tab, s_tab, hw)
    yre, yim = _fwd_dft(u, u_row0, 2, kre, kim, 0, c_tab, s_tab, b, l, hw)
    z1 = _inv_dft(yre, yim, u, u_row0, 0, u, u_row0, 2, p["hy_bias"], layer, 0, c_tab, st_tab, b, l, hw, F32)
    yre, yim = _fwd_dft(z1, 0, 0, kre, kim, 1, c_tab, s_tab, b, l, hw)
    return _inv_dft(yre, yim, u, u_row0, 1, z1, 0, 0, p["hy_bias"], layer, 1, c_tab, st_tab, b, l, hw, BF16)


def _merge_kernel(attl_ref, attc_ref, conf_ref, hyl_ref, hyc_ref, wa_ref, wc_ref, wh_ref, g0_ref, g1_ref, g2_ref,
                  o_ref, *, n_lat):
    is_lat = pl.program_id(0) < n_lat
    a = _dot(jnp.where(is_lat, attl_ref[...], attc_ref[...]), wa_ref[...].astype(BF16))
    c = _dot(conf_ref[...], wc_ref[...].astype(BF16))
    h = _dot(jnp.where(is_lat, hyl_ref[...], hyc_ref[...]), wh_ref[...].astype(BF16))
    m = g0_ref[...].astype(F32) * a + g1_ref[...].astype(F32) * c + g2_ref[...].astype(F32) * h
    o_ref[...] = m.astype(o_ref.dtype)


def _merge(att, att_ctx, conf, hy, hy_ctx, gates, w_attn_o, w_conf_o, w_hy_o, layer, n_rows, n_lat_rows, d, tm, tn):
    aw, cw, hw = att.shape[1], conf.shape[1], hy.shape[1]
    gb = d // tn
    n_lat = n_lat_rows // tm
    lat_map = lambda i, j: (jnp.minimum(i, n_lat - 1), 0)
    ctx_map = lambda i, j: (jnp.maximum(i - n_lat, 0), 0)
    return pl.pallas_call(
        functools.partial(_merge_kernel, n_lat=n_lat),
        out_shape=jax.ShapeDtypeStruct((n_rows, d), BF16),
        grid=(n_rows // tm, d // tn),
        in_specs=[
            pl.BlockSpec((tm, aw), lat_map),
            pl.BlockSpec((tm, aw), ctx_map),
            pl.BlockSpec((tm, cw), lambda i, j: (i, 0)),
            pl.BlockSpec((tm, hw), lat_map),
            pl.BlockSpec((tm, hw), ctx_map),
            pl.BlockSpec((None, aw, tn), lambda i, j: (layer, 0, j)),
            pl.BlockSpec((None, cw, tn), lambda i, j: (layer, 0, j)),
            pl.BlockSpec((None, hw, tn), lambda i, j: (layer, 0, j)),
            pl.BlockSpec((tm, tn), lambda i, j: (i, j)),
            pl.BlockSpec((tm, tn), lambda i, j: (i, gb + j)),
            pl.BlockSpec((tm, tn), lambda i, j: (i, 2 * gb + j)),
        ],
        out_specs=pl.BlockSpec((tm, tn), lambda i, j: (i, j)),
        compiler_params=_params("parallel", "arbitrary"),
        name="branch_proj_merge",
    )(att, att_ctx, conf, hy, hy_ctx, w_attn_o, w_conf_o, w_hy_o, gates, gates, gates)


def _out_proj_kernel(a_ref, w_ref, x_ref, g_ref, o_ref):
    o_ref[...] = x_ref[...] + g_ref[...] * _dot(a_ref[...], w_ref[...].astype(BF16))


def _out_proj(merged, w_out, x_all, mod3, layer, n_rows, dims, tm, tn):
    d = x_all.shape[1]
    n_lat = dims["T"] // tm
    mm = functools.partial(_mod_row_map, layer, tm, n_lat, dims["S"], dims["B"])

    def gate_map(i, j):
        row, z, _ = mm(0)(i)
        return (row, z, 2 * (d // tn) + j)

    return pl.pallas_call(
        _out_proj_kernel,
        out_shape=jax.ShapeDtypeStruct((n_rows, d), F32),
        grid=(n_rows // tm, d // tn),
        in_specs=[
            pl.BlockSpec((tm, d), lambda i, j: (i, 0)),
            pl.BlockSpec((None, d, tn), lambda i, j: (layer, 0, j)),
            pl.BlockSpec((tm, tn), lambda i, j: (i, j)),
            pl.BlockSpec((None, 1, tn), gate_map),
        ],
        out_specs=pl.BlockSpec((tm, tn), lambda i, j: (i, j)),
        compiler_params=_params("parallel", "arbitrary"),
        name="out_proj_residual",
    )(merged, w_out, x_all, mod3)


def _store_token_tiled(ref, x):
    n, d = x.shape
    ns = d // LANES
    for s in range(ns):
        ref[pl.ds(s, n, stride=ns), :] = x[:, s * LANES:(s + 1) * LANES]


def _router_kernel(x_ref, g_ref, sh_ref, sc_ref, wr_ref, h_ref, r_ref):
    h = _ada_norm_value(x_ref[...], g_ref[...], sh_ref[...], sc_ref[...])
    _store_token_tiled(h_ref, h)
    logits = _dot3(h, wr_ref[...])
    lane = lax.broadcasted_iota(jnp.int32, logits.shape, 1).astype(F32)
    ninf = jnp.float32(-jnp.inf)
    big = jnp.float32(LANES)

    def first_argmax(v):
        mx = jnp.max(v, axis=-1, keepdims=True)
        idx = jnp.min(jnp.where(v == mx, lane, big), axis=-1, keepdims=True)
        return mx, idx

    lg = jnp.where(lane < N_GROUPS, logits, ninf)
    gmax, gidx = first_argmax(lg)
    p_grp = 1.0 / jnp.sum(jnp.exp(lg - gmax), axis=-1, keepdims=True)
    lo = N_GROUPS + gidx * E_PER_GROUP
    le = jnp.where(jnp.logical_and(lane >= lo, lane < lo + E_PER_GROUP), logits, ninf)
    e1, i1 = first_argmax(le)
    e2, i2 = first_argmax(jnp.where(lane == i1, ninf, le))
    t = jnp.exp(e2 - e1)
    w1 = p_grp / (1.0 + t)
    w2 = p_grp * t / (1.0 + t)
    out = jnp.where(lane == 0, i1 - N_GROUPS,
                    jnp.where(lane == 1, i2 - N_GROUPS,
                              jnp.where(lane == 2, w1, jnp.where(lane == 3, w2, 0.0))))
    r_ref[...] = out


def _router(x_all, norm_g, mod3, w_router, layer, n_rows, dims, tm):
    d = x_all.shape[1]
    ns = d // LANES
    n_lat = dims["T"] // tm
    mm = functools.partial(_mod_row_map, layer, tm, n_lat, dims["S"], dims["B"])
    return pl.pallas_call(
        _router_kernel,
        out_shape=(jax.ShapeDtypeStruct((n_rows * ns, LANES), F32), jax.ShapeDtypeStruct((n_rows, LANES), F32)),
        grid=(n_rows // tm,),
        in_specs=[
            pl.BlockSpec((tm, d), lambda i: (i, 0)),
            pl.BlockSpec((None, 1, d), lambda i: (layer, 0, 0)),
            pl.BlockSpec((None, 1, d), mm(3)),
            pl.BlockSpec((None, 1, d), mm(4)),
            pl.BlockSpec((d, LANES), lambda i: (0, 0)),
        ],
        out_specs=(pl.BlockSpec((tm * ns, LANES), lambda i: (i, 0)), pl.BlockSpec((tm, LANES), lambda i: (i, 0))),
        compiler_params=_params("parallel"),
        name="adanorm2_router",
    )(x_all, norm_g.reshape(norm_g.shape[0], 1, d), mod3, mod3, w_router)


def _dispatch(route, n_tok, bm):
    eid = route[:, 0:2].astype(jnp.int32).reshape(-1)
    a = 2 * n_tok
    e_s, order = lax.sort_key_val(eid, jnp.arange(a, dtype=jnp.int32))
    experts = jnp.arange(N_EXPERTS, dtype=jnp.int32)
    cstart = jnp.searchsorted(e_s, experts, side='left').astype(jnp.int32)
    counts = jnp.searchsorted(e_s, experts, side='right').astype(jnp.int32) - cstart
    pcounts = (counts + bm - 1) // bm * bm
    pend = jnp.cumsum(pcounts)
    pstart = pend - pcounts
    n_blocks = -(-a // bm) + N_EXPERTS
    blk = jnp.arange(n_blocks, dtype=jnp.int32)
    blk_e = jnp.minimum(jnp.searchsorted(pend, blk * bm, side='right'), N_EXPERTS - 1).astype(jnp.int32)
    k = blk - pstart[blk_e] // bm
    blk_src0 = jnp.clip(cstart[blk_e] + k * bm, 0, a - 1).astype(jnp.int32)
    blk_nvalid = jnp.clip(counts[blk_e] - k * bm, 0, bm).astype(jnp.int32)
    n_used = (pend[-1] // bm).astype(jnp.int32).reshape(1)
    return order, blk_e, blk_src0, blk_nvalid, n_used, n_blocks


def _token_copy(src, src_tok, dst, dst_tok, ns, sem):
    return pltpu.make_async_copy(src.at[pl.ds(pl.multiple_of(src_tok * ns, ns), ns), :],
                                 dst.at[pl.ds(pl.multiple_of(dst_tok * ns, ns), ns), :], sem)


def _expert_kernel(tok_ref, order_ref, be_ref, src0_ref, nvalid_ref, nused_ref, h_hbm, wg_ref, wu_ref, wd_ref,
                   y_hbm, wg_s, wu_s, wd_s, xbuf, xb, obuf, gsem, ssem, *, bm, ns, n_assign, n_blocks):
    i = pl.program_id(0)
    slot = i % 2
    n_used = nused_ref[0]
    changed = jnp.logical_or(i == 0, be_ref[i] != be_ref[jnp.maximum(i - 1, 0)])

    def gather(block, sl):
        base = src0_ref[block]

        def issue(r, carry):
            _token_copy(h_hbm, tok_ref[base + r], xbuf.at[sl], r, ns, gsem.at[sl]).start()
            return carry
        lax.fori_loop(0, bm, issue, 0, unroll=True)

    def wait_all(buf, hbm, sem, sl):
        pltpu.make_async_copy(buf.at[sl], hbm.at[pl.ds(0, bm * ns), :], sem.at[sl]).wait()

    @pl.when(i == 0)
    def _():
        gather(0, 0)

    @pl.when(i + 1 < n_used)
    def _():
        gather(i + 1, 1 - slot)

    @pl.when(i >= 2)
    def _():
        wait_all(obuf, y_hbm, ssem, slot)

    @pl.when(i < n_used)
    def _():
        wait_all(xbuf, h_hbm, gsem, slot)
        for s in range(ns):
            xb[:, s * LANES:(s + 1) * LANES] = xbuf[slot, pl.ds(s, bm, stride=ns), :].astype(BF16)

        @pl.when(changed)
        def _():
            wg_s[...] = wg_ref[...].astype(BF16)
            wu_s[...] = wu_ref[...].astype(BF16)
            wd_s[...] = wd_ref[...].astype(BF16)

        x = xb[...]
        g = _dot(x, wg_s[...])
        u = _dot(x, wu_s[...])
        mid = (g * jax.nn.sigmoid(g) * u).astype(BF16)
        _store_token_tiled(obuf.at[slot], _dot(mid, wd_s[...]))

    @pl.when(i >= n_used)
    def _():
        obuf[slot] = jnp.zeros(obuf.shape[1:], obuf.dtype)

    base = src0_ref[i]
    n_real = nvalid_ref[i]

    def issue_out(r, carry):
        dst = jnp.where(r < n_real, order_ref[base + r], n_assign + slot * bm + r)
        _token_copy(obuf.at[slot], r, y_hbm, dst, ns, ssem.at[slot]).start()
        return carry
    lax.fori_loop(0, bm, issue_out, 0, unroll=True)

    @pl.when(i == n_blocks - 1)
    def _():
        if n_blocks > 1:
            wait_all(obuf, y_hbm, ssem, 1 - slot)
        wait_all(obuf, y_hbm, ssem, slot)


def _experts(h2t, order, blk_e, blk_src0, blk_nvalid, n_used, w_gate, w_up, w_down, layer, n_blocks, bm, d):
    f = w_gate.shape[3]
    ns = d // LANES
    n_assign = order.shape[0]
    order_p = jnp.concatenate([order, jnp.zeros((bm,), jnp.int32)])
    tok_p = order_p // 2
    wspec = lambda shape: pl.BlockSpec((None, None) + shape, lambda i, t, o, be, *_: (layer, be[i], 0, 0))
    return pl.pallas_call(
        functools.partial(_expert_kernel, bm=bm, ns=ns, n_assign=n_assign, n_blocks=n_blocks),
        out_shape=jax.ShapeDtypeStruct(((n_assign + 2 * bm) * ns, LANES), F32),
        grid_spec=pltpu.PrefetchScalarGridSpec(
            num_scalar_prefetch=6,
            grid=(n_blocks,),
            in_specs=[pl.BlockSpec(memory_space=pl.ANY), wspec((d, f)), wspec((d, f)), wspec((f, d))],
            out_specs=pl.BlockSpec(memory_space=pl.ANY),
            scratch_shapes=[pltpu.VMEM((d, f), BF16), pltpu.VMEM((d, f), BF16), pltpu.VMEM((f, d), BF16),
                            pltpu.VMEM((2, bm * ns, LANES), F32), pltpu.VMEM((bm, d), BF16),
                            pltpu.VMEM((2, bm * ns, LANES), F32),
                            pltpu.SemaphoreType.DMA((2,)), pltpu.SemaphoreType.DMA((2,))],
        ),
        compiler_params=_params("arbitrary"),
        name="moe_experts",
    )(tok_p, order_p, blk_e, blk_src0, blk_nvalid, n_used, h2t, w_gate, w_up, w_down)


def _combine_kernel(y_ref, x_ref, g_ref, route_ref, *rest, tc, ns, final):
    if final:
        ng_ref, o_ref = rest
    else:
        (o_ref,) = rest
    w0 = route_ref[:, 2:3]
    w1 = route_ref[:, 3:4]
    ssq = jnp.zeros((tc, 1), F32)
    for s in range(ns):
        sl = slice(s * LANES, (s + 1) * LANES)
        y = w0 * y_ref[pl.ds(s, tc, stride=2 * ns), :] + w1 * y_ref[pl.ds(ns + s, tc, stride=2 * ns), :]
        xn = x_ref[:, sl] + g_ref[:, sl] * y
        if final:
            ssq = ssq + jnp.sum(xn * xn, axis=-1, keepdims=True)
        o_ref[:, sl] = xn
    if final:
        o_ref[...] = o_ref[...] * lax.rsqrt(ssq / (ns * LANES) + EPS) * ng_ref[...]


def _combine(ys, route, x_all, mod3, layer, n_rows, dims, tc, norm_f_g=None):
    d = x_all.shape[1]
    ns = d // LANES
    n_lat = dims["T"] // tc
    mm = functools.partial(_mod_row_map, layer, tc, n_lat, dims["S"], dims["B"])
    final = norm_f_g is not None
    in_specs = [
        pl.BlockSpec((tc * 2 * ns, LANES), lambda i: (i, 0)),
        pl.BlockSpec((tc, d), lambda i: (i, 0)),
        pl.BlockSpec((None, 1, d), mm(5)),
        pl.BlockSpec((tc, LANES), lambda i: (i, 0)),
    ]
    args = [ys, x_all, mod3, route]
    if final:
        in_specs.append(pl.BlockSpec((1, d), lambda i: (0, 0)))
        args.append(norm_f_g.reshape(1, d))
    return pl.pallas_call(
        functools.partial(_combine_kernel, tc=tc, ns=ns, final=final),
        out_shape=jax.ShapeDtypeStruct((n_rows, d), F32),
        grid=(n_rows // tc,),
        in_specs=in_specs,
        out_specs=pl.BlockSpec((tc, d), lambda i: (i, 0)),
        compiler_params=_params("parallel"),
        name="moe_combine_residual",
    )(*args)


def _rope_table(s):
    half = ATT_DQK // 2
    rows = jnp.repeat(jnp.arange(s // GRID_W, dtype=jnp.int32), GRID_W)
    cols = jnp.tile(jnp.arange(GRID_W, dtype=jnp.int32), s // GRID_W)
    inv = ROPE_BASE ** (-jnp.arange(0, half, 2, dtype=F32) / half)
    ar = rows.astype(F32)[:, None] * inv
    ac = cols.astype(F32)[:, None] * inv
    cr, sr, cc, sc = jnp.cos(ar), jnp.sin(ar), jnp.cos(ac), jnp.sin(ac)
    z = jnp.zeros_like(sr)
    c = jnp.concatenate([cr, cr, cc, cc] * 2, axis=-1)
    s1 = jnp.concatenate([-sr, z, -sc, z] * 2, axis=-1)
    s2 = jnp.concatenate([z, sr, z, sc] * 2, axis=-1)
    return jnp.concatenate([c, s1, s2], axis=-1)


def kernel(x, c, ctx, c_ctx, w_mod, b_mod, norm1_g, norm2_g, w_in, lam_q1, lam_k1, lam_q2, lam_k2, attn_subln_g, w_attn_o, conf_dw_w, conf_dw_b, conf_ln_g, conf_ln_b, w_conf_o, hy_sc_w, hy_sc_b, hy_w1, hy_b1, hy_w2, hy_b2, hy_freq, hy_w3, hy_decay, hy_bias, w_hy_o, w_out, w_router_group, w_router_expert, w_exp_gate, w_exp_up, w_exp_down, norm_f_g):
    b, s, d = x.shape
    lc = ctx.shape[1]
    depth = w_mod.shape[0]
    t, tc_rows = b * s, b * lc
    cw, hw = conf_dw_w.shape[2], w_hy_o.shape[1]
    dims = dict(B=b, S=s, Lc=lc, T=t, QK=ATT_HEADS * 2 * ATT_DQK, AW=ATT_HEADS * ATT_DV, CW=cw, HW=hw)
    assert b + 1 <= MOD_ROWS and 2 * ATT_DQK == LANES and ATT_DV == LANES and hw == cw
    tm = _largest_tile((1024, 512, 256, 128), s, tc_rows)
    ts = _largest_tile((256, 128), s, lc)
    tn = _largest_tile((512, 256, 128), dims["QK"], dims["AW"], 2 * cw, 3 * hw, d)
    tcomb = _largest_tile((256, 128), s, tc_rows)

    x_all = jnp.concatenate([x.reshape(t, d), ctx.reshape(tc_rows, d)], axis=0)
    cond = jnp.zeros((MOD_ROWS, d), F32).at[:b].set(c).at[b].set(c_ctx)
    mod3 = _modulation(cond, w_mod, b_mod).reshape(depth * MOD_ROWS, 1, 6 * d)
    rope_tab = _rope_table(s)
    tables_lat, emb_lat = _dft_tables(s), _hyena_embedding(s)
    tables_ctx, emb_ctx = _dft_tables(lc), _hyena_embedding(lc)
    hy_params = dict(hy_w1=hy_w1, hy_b1=hy_b1, hy_w2=hy_w2, hy_b2=hy_b2, hy_freq=hy_freq, hy_w3=hy_w3,
                     hy_decay=hy_decay, hy_bias=hy_bias)

    out = None
    for l in range(depth):
        need_ctx = l < depth - 1
        n_rows = t + tc_rows if need_ctx else t
        lam_init = 0.8 - 0.6 * math.exp(-0.3 * l)
        lam = (jnp.exp(jnp.sum(lam_q1[l] * lam_k1[l])) - jnp.exp(jnp.sum(lam_q2[l] * lam_k2[l])) + lam_init)
        lam = lam.reshape(1).astype(F32)

        hl = _adanorm(x_all, norm1_g, mod3, l, dims, tm)
        qkv, conv_in, gates = _in_proj(hl, w_in, l, rope_tab, dims, tm, tn)
        att, att_ctx = _attention(qkv, lam, attn_subln_g, l, lam_init, dims, need_ctx)
        conf = _conformer(conv_in, conf_dw_w, conf_dw_b, conf_ln_g, conf_ln_b, l, dims, n_rows, ts)
        u = _hyena_short(conv_in, hy_sc_w, hy_sc_b, l, dims, n_rows, ts)
        hy = _hyena_long(u, 0, b, s, l, hy_params, tables_lat, emb_lat, dims)
        hy_ctx = _hyena_long(u, t, b, lc, l, hy_params, tables_ctx, emb_ctx, dims) if need_ctx else hy
        merged = _merge(att, att_ctx, conf, hy, hy_ctx, gates, w_attn_o, w_conf_o, w_hy_o, l, n_rows, t, d, tm, tn)
        x_all = _out_proj(merged, w_out, x_all, mod3, l, n_rows, dims, tm, tn)

        w_router = jnp.concatenate(
            [w_router_group[l], jnp.transpose(w_router_expert[l], (1, 0, 2)).reshape(d, N_EXPERTS)], axis=1)
        w_router = jnp.pad(w_router, ((0, 0), (0, LANES - w_router.shape[1])))
        h2t, route = _router(x_all, norm2_g, mod3, w_router, l, n_rows, dims, tm)
        order, blk_e, blk_src0, blk_nvalid, n_used, n_blocks = _dispatch(route, n_rows, MOE_BLOCK)
        ys = _experts(h2t, order, blk_e, blk_src0, blk_nvalid, n_used, w_exp_gate, w_exp_up, w_exp_down, l,
                      n_blocks, MOE_BLOCK, d)
        x_new = _combine(ys, route, x_all, mod3, l, n_rows, dims, tcomb, None if need_ctx else norm_f_g)
        if need_ctx:
            x_all = x_new
        else:
            out = x_new
    return out.reshape(b, s, d)
```

```python
import functools
import math

import jax
import jax.numpy as jnp
from jax import lax
from jax.experimental import pallas as pl
from jax.experimental.pallas import tpu as pltpu

F32 = jnp.float32
BF16 = jnp.bfloat16

GRID_W = 64
EPS = 1e-6
ATT_HEADS = 8
ATT_DQK = 64
ATT_DV = 2 * ATT_DQK
ROPE_BASE = 10000.0
CONF_K = 31
HY_ORDER = 2
HY_SHORT_K = 3
HY_EMB = 33
HY_BANDS = (HY_EMB - 1) // 2
N_BRANCH = 3
N_GROUPS = 8
E_PER_GROUP = 8
N_EXPERTS = N_GROUPS * E_PER_GROUP

LANES = 128
SUBLANES = 8
VMEM_LIMIT_BYTES = 56 * 1024 * 1024

CONF_HALO = 16
HY_HALO = 8
MOE_BLOCK = 256
ATT_UNROLL = 16
ATT_TQ = (2048, 1024, 512, 256, 128)
ATT_TK = (256, 128)
MOD_ROWS = 8


def _params(*sem):
    return pltpu.CompilerParams(dimension_semantics=sem, vmem_limit_bytes=VMEM_LIMIT_BYTES)


def _largest_tile(cands, *dims):
    for c in cands:
        if all(d % c == 0 for d in dims):
            return c
    raise ValueError(f"no tile in {cands} divides {dims}")


def _dot(a, b):
    return jnp.dot(a, b, preferred_element_type=F32)


def _split_bf16(x):
    hi = x.astype(BF16)
    lo = (x - hi.astype(F32)).astype(BF16)
    return hi, lo


def _dot3(a, b):
    ah, al = _split_bf16(a)
    bh, bl = _split_bf16(b)
    return _dot(ah, bh) + _dot(al, bh) + _dot(ah, bl)


def _mod_kernel(c_ref, w_ref, b_ref, o_ref):
    c = c_ref[...]
    s = c * jax.nn.sigmoid(c)
    o_ref[...] = _dot3(s, w_ref[...]) + b_ref[...]


def _modulation(cond, w_mod, b_mod):
    depth, d, n = w_mod.shape
    tn = _largest_tile((512, 256, 128), n)
    return pl.pallas_call(
        _mod_kernel,
        out_shape=jax.ShapeDtypeStruct((depth, MOD_ROWS, n), F32),
        grid=(depth, n // tn),
        in_specs=[
            pl.BlockSpec((MOD_ROWS, d), lambda l, j: (0, 0)),
            pl.BlockSpec((None, d, tn), lambda l, j: (l, 0, j)),
            pl.BlockSpec((None, 1, tn), lambda l, j: (l, 0, j)),
        ],
        out_specs=pl.BlockSpec((None, MOD_ROWS, tn), lambda l, j: (l, 0, j)),
        compiler_params=_params("parallel", "parallel"),
        name="modulation",
    )(cond, w_mod, b_mod.reshape(depth, 1, n))


def _ada_norm_value(x, g, shift, scale):
    ms = jnp.mean(x * x, axis=-1, keepdims=True)
    y = x * lax.rsqrt(ms + EPS) * g
    return y * (1.0 + scale) + shift


def _adanorm_kernel(xa_ref, xb_ref, g_ref, sh_ref, sc_ref, o_ref, *, n_first):
    x = jnp.where(pl.program_id(0) < n_first, xa_ref[...], xb_ref[...])
    o_ref[...] = _ada_norm_value(x, g_ref[...], sh_ref[...], sc_ref[...]).astype(o_ref.dtype)


def _mod_row_map(layer, tm, n_lat, s, b, col):
    def index_map(i, *_):
        row = jnp.where(i < n_lat, (i * tm) // s, b)
        return (layer * MOD_ROWS + row, 0, col)
    return index_map


def _adanorm(x_pair, n_rows, norm_g, mod3, layer, dims, tm):
    xa, xb = x_pair
    r, d = n_rows, xa.shape[1]
    n_lat = dims["T"] // tm
    n_first = xa.shape[0] // tm
    mm = functools.partial(_mod_row_map, layer, tm, n_lat, dims["S"], dims["B"])
    return pl.pallas_call(
        functools.partial(_adanorm_kernel, n_first=n_first),
        out_shape=jax.ShapeDtypeStruct((r, d), BF16),
        grid=(r // tm,),
        in_specs=[
            pl.BlockSpec((tm, d), lambda i: (jnp.minimum(i, n_first - 1), 0)),
            pl.BlockSpec((tm, d), lambda i: (jnp.maximum(i - n_first, 0), 0)),
            pl.BlockSpec((None, 1, d), lambda i: (layer, 0, 0)),
            pl.BlockSpec((None, 1, d), mm(0)),
            pl.BlockSpec((None, 1, d), mm(1)),
        ],
        out_specs=pl.BlockSpec((tm, d), lambda i: (i, 0)),
        compiler_params=_params("parallel"),
        name="adanorm1",
    )(xa, xb, norm_g.reshape(norm_g.shape[0], 1, d), mod3, mod3)


def _qkv_kernel(a_ref, w_ref, tab_ref, o_ref, *, n_lat, tn, qk_cols):
    i = pl.program_id(0)
    j = pl.program_id(1)
    acc = _dot(a_ref[...], w_ref[...].astype(BF16))
    col0 = j * tn
    scale = jnp.where(col0 < qk_cols, ATT_DQK ** -0.5 * math.log2(math.e), 1.0).astype(F32)
    do_rope = jnp.logical_and(i < n_lat, col0 < 2 * qk_cols)

    @pl.when(do_rope)
    def _():
        c = tab_ref[:, 0:LANES]
        s1 = tab_ref[:, LANES:2 * LANES]
        s2 = tab_ref[:, 2 * LANES:3 * LANES]
        for h in range(tn // LANES):
            xh = acc[:, h * LANES:(h + 1) * LANES] * scale
            rot = xh * c + pltpu.roll(xh, LANES - 16, 1) * s1 + pltpu.roll(xh, 16, 1) * s2
            o_ref[:, h * LANES:(h + 1) * LANES] = rot.astype(o_ref.dtype)

    @pl.when(jnp.logical_not(do_rope))
    def _():
        o_ref[...] = (acc * scale).astype(o_ref.dtype)


def _plain_mm_kernel(a_ref, w_ref, o_ref):
    o_ref[...] = _dot(a_ref[...], w_ref[...].astype(BF16)).astype(o_ref.dtype)


def _sigmoid_mm_kernel(a_ref, w_ref, o_ref):
    w = w_ref[...].astype(BF16)
    half = o_ref.shape[0] // 2
    for c in range(2):
        rows = slice(c * half, (c + 1) * half)
        o_ref[rows, :] = jax.nn.sigmoid(_dot(a_ref[rows, :], w)).astype(o_ref.dtype)


def _in_proj(hl, w_in, layer, rope_tab, dims, tm, tn):
    r, d = hl.shape
    qk, aw, cw, hw = dims["QK"], dims["AW"], dims["CW"], dims["HW"]
    n_lat = dims["T"] // tm
    s_tiles = dims["S"] // tm
    a_spec = pl.BlockSpec((tm, d), lambda i, j: (i, 0))

    def w_spec(col_off):
        off = col_off // tn
        return pl.BlockSpec((None, d, tn), lambda i, j: (layer, 0, j + off))

    def call(kernel, col_off, n_cols, dtype, name, extra_in=(), extra_specs=()):
        return pl.pallas_call(
            kernel,
            out_shape=jax.ShapeDtypeStruct((r, n_cols), dtype),
            grid=(r // tm, n_cols // tn),
            in_specs=[a_spec, w_spec(col_off), *extra_specs],
            out_specs=pl.BlockSpec((tm, tn), lambda i, j: (i, j)),
            compiler_params=_params("parallel", "arbitrary"),
            name=name,
        )(hl, w_in, *extra_in)

    qkv = call(
        functools.partial(_qkv_kernel, n_lat=n_lat, tn=tn, qk_cols=qk), 0, 2 * qk + aw, BF16, "in_proj_qkv",
        extra_in=(rope_tab,),
        extra_specs=(pl.BlockSpec((tm, 3 * LANES), lambda i, j: (i % s_tiles, 0)),))
    conv_in = call(_plain_mm_kernel, 2 * qk + aw, 2 * cw + 3 * hw, F32, "in_proj_conv")
    gates = call(_sigmoid_mm_kernel, 2 * qk + aw + 2 * cw + 3 * hw, N_BRANCH * d, BF16, "in_proj_gates")
    return qkv, conv_in, gates


def _attn_kernel(lam_ref, q_ref, g_ref, kc_ref, vc_ref, *rest, n_lat_chunks, tk, out_scale, unroll):
    if n_lat_chunks:
        kl_ref, vl_ref, o_ref, vce, vle = rest
    else:
        o_ref, vce = rest

    @pl.when(pl.program_id(2) == 0)
    def _():
        vce[:, 0:ATT_DV] = vc_ref[...]
        vce[:, ATT_DV:] = jnp.ones((vce.shape[0], ATT_DV), BF16)
        if n_lat_chunks:
            vle[:, 0:ATT_DV] = vl_ref[...]
            vle[:, ATT_DV:] = jnp.ones((vle.shape[0], ATT_DV), BF16)

    q = q_ref[...]
    lane = lax.broadcasted_iota(jnp.int32, q.shape, 1)
    zero = jnp.zeros_like(q)
    q1 = jnp.where(lane < ATT_DQK, q, zero)
    q2 = jnp.where(lane >= ATT_DQK, q, zero)
    tq = q.shape[0]
    nt = (((1,), (1,)), ((), ()))

    def one_map(qm, k, v, m, acc):
        s = lax.dot_general(qm, k, nt, preferred_element_type=F32)
        m_new = jnp.maximum(m, jnp.max(s, axis=-1, keepdims=True))
        alpha = jnp.exp2(m - m_new)
        p = jnp.exp2((s - m_new).astype(BF16))
        return m_new, alpha * acc + _dot(p, v)

    def step(k, v, carry):
        m1, a1, m2, a2 = carry
        m1, a1 = one_map(q1, k, v, m1, a1)
        m2, a2 = one_map(q2, k, v, m2, a2)
        return m1, a1, m2, a2

    neg = jnp.full((tq, 1), -jnp.inf, F32)
    za = jnp.zeros((tq, 2 * ATT_DV), F32)
    carry = step(kc_ref[...], vce[...], (neg, za, neg, za))
    if n_lat_chunks:
        def body(c, carry):
            off = pl.multiple_of(c * tk, tk)
            return step(kl_ref[pl.ds(off, tk), :], vle[pl.ds(off, tk), :], carry)
        carry = lax.fori_loop(0, n_lat_chunks, body, carry, unroll=unroll)
    _, a1, _, a2 = carry
    o = a1[:, 0:ATT_DV] / a1[:, ATT_DV:] - lam_ref[0] * (a2[:, 0:ATT_DV] / a2[:, ATT_DV:])
    ms = jnp.mean(o * o, axis=-1, keepdims=True)
    o = o * lax.rsqrt(ms + 1e-5) * g_ref[...] * out_scale
    o_ref[...] = o.astype(o_ref.dtype)


def _attention(qkv, lam, subln_g, layer, lam_init, dims, with_ctx):
    b, s, lc, t = dims["B"], dims["S"], dims["Lc"], dims["T"]
    h = ATT_HEADS
    tq = _largest_tile(ATT_TQ, s)
    tk = _largest_tile(ATT_TK, s)
    tqc = _largest_tile((512, 256, 128), lc)
    smem = pl.BlockSpec(memory_space=pltpu.SMEM)
    g3 = subln_g.reshape(subln_g.shape[0], 1, ATT_DV)
    g_spec = pl.BlockSpec((None, 1, ATT_DV), lambda bi, hi, ti: (layer, 0, 0))
    kc_spec = pl.BlockSpec((lc, LANES), lambda bi, hi, ti: (t // lc + bi, h + hi))
    vc_spec = pl.BlockSpec((lc, LANES), lambda bi, hi, ti: (t // lc + bi, 2 * h + hi))
    out_scale = 1.0 - lam_init

    sem = ("arbitrary", "arbitrary", "arbitrary")
    att = pl.pallas_call(
        functools.partial(_attn_kernel, n_lat_chunks=s // tk, tk=tk, out_scale=out_scale,
                          unroll=min(ATT_UNROLL, s // tk)),
        out_shape=jax.ShapeDtypeStruct((t, h * ATT_DV), BF16),
        grid=(b, h, s // tq),
        in_specs=[
            smem,
            pl.BlockSpec((tq, LANES), lambda bi, hi, ti: (bi * (s // tq) + ti, hi)),
            g_spec, kc_spec, vc_spec,
            pl.BlockSpec((s, LANES), lambda bi, hi, ti: (bi, h + hi)),
            pl.BlockSpec((s, LANES), lambda bi, hi, ti: (bi, 2 * h + hi)),
        ],
        out_specs=pl.BlockSpec((tq, LANES), lambda bi, hi, ti: (bi * (s // tq) + ti, hi)),
        scratch_shapes=[pltpu.VMEM((lc, 2 * ATT_DV), BF16), pltpu.VMEM((s, 2 * ATT_DV), BF16)],
        compiler_params=_params(*sem),
        name="diff_attn_latent",
    )(lam, qkv, g3, qkv, qkv, qkv, qkv)
    if not with_ctx:
        return att, att

    att_ctx = pl.pallas_call(
        functools.partial(_attn_kernel, n_lat_chunks=0, tk=tk, out_scale=out_scale, unroll=1),
        out_shape=jax.ShapeDtypeStruct((b * lc, h * ATT_DV), BF16),
        grid=(b, h, lc // tqc),
        in_specs=[
            smem,
            pl.BlockSpec((tqc, LANES), lambda bi, hi, ti: ((t + bi * lc) // tqc + ti, hi)),
            g_spec, kc_spec, vc_spec,
        ],
        out_specs=pl.BlockSpec((tqc, LANES), lambda bi, hi, ti: (bi * (lc // tqc) + ti, hi)),
        scratch_shapes=[pltpu.VMEM((lc, 2 * ATT_DV), BF16)],
        compiler_params=_params(*sem),
        name="diff_attn_ctx",
    )(lam, qkv, g3, qkv, qkv)
    return att, att_ctx


def _seq_edges(i, n_lat, tps_lat, tps_ctx):
    is_lat = i < n_lat
    pos = jnp.where(is_lat, i % tps_lat, (i - n_lat) % tps_ctx)
    n = jnp.where(is_lat, tps_lat, tps_ctx)
    return pos == 0, pos == n - 1


def _halo_specs(ts, halo, width, col, n_rows):
    per = ts // halo
    last = n_rows // halo - 1
    prev = pl.BlockSpec((halo, width), lambda i, *_: (jnp.maximum(i * per - 1, 0), col(*_)))
    cur = pl.BlockSpec((ts, width), lambda i, *_: (i, col(*_)))
    nxt = pl.BlockSpec((halo, width), lambda i, *_: (jnp.minimum((i + 1) * per, last), col(*_)))
    return prev, cur, nxt


def _conf_kernel(ap_ref, ac_ref, an_ref, gp_ref, gc_ref, gn_ref, w_ref, b_ref, lg_ref, lb_ref, o_ref, ybuf, ysh,
                 *, ts, n_lat, tps_lat, tps_ctx, row_chunk):
    first, last = _seq_edges(pl.program_id(0), n_lat, tps_lat, tps_ctx)
    halo = CONF_HALO

    def glu(a, g):
        return a * jax.nn.sigmoid(g)

    prev = glu(ap_ref[...], gp_ref[...])
    nxt = glu(an_ref[...], gn_ref[...])
    ybuf[0:halo, :] = jnp.where(first, jnp.zeros_like(prev), prev)
    ybuf[halo:halo + ts, :] = glu(ac_ref[...], gc_ref[...])
    ybuf[halo + ts:, :] = jnp.where(last, jnp.zeros_like(nxt), nxt)
    n_sh = ysh.shape[1]
    for b in range(1, SUBLANES):
        ysh[b] = ybuf[b:b + n_sh, :]
    base = halo - CONF_K // 2
    for r0 in range(0, ts, row_chunk):
        acc = jnp.zeros((row_chunk, ybuf.shape[1]), F32) + b_ref[...]
        for j in range(CONF_K):
            b = (base + j) % SUBLANES
            start = r0 + base + j - b
            rows = ybuf[start:start + row_chunk, :] if b == 0 else ysh[b, start:start + row_chunk, :]
            acc = acc + w_ref[j:j + 1, :] * rows
        mu = jnp.mean(acc, axis=-1, keepdims=True)
        xc = acc - mu
        var = jnp.mean(xc * xc, axis=-1, keepdims=True)
        y = xc * lax.rsqrt(var + 1e-5) * lg_ref[...] + lb_ref[...]
        o_ref[r0:r0 + row_chunk, :] = (y * jax.nn.sigmoid(y)).astype(o_ref.dtype)


def _conformer(conv_in, dw_w, dw_b, ln_g, ln_b, layer, dims, n_rows, ts):
    cw = dims["CW"]
    depth = dw_w.shape[0]
    n_lat = dims["T"] // ts
    a_specs = _halo_specs(ts, CONF_HALO, cw, lambda: 0, conv_in.shape[0])
    g_specs = _halo_specs(ts, CONF_HALO, cw, lambda: 1, conv_in.shape[0])
    vec = lambda: pl.BlockSpec((None, 1, cw), lambda i: (layer, 0, 0))
    kern = functools.partial(_conf_kernel, ts=ts, n_lat=n_lat, tps_lat=dims["S"] // ts, tps_ctx=dims["Lc"] // ts,
                             row_chunk=min(32, ts))
    return pl.pallas_call(
        kern,
        out_shape=jax.ShapeDtypeStruct((n_rows, cw), BF16),
        grid=(n_rows // ts,),
        in_specs=[*a_specs, *g_specs,
                  pl.BlockSpec((None, CONF_K, cw), lambda i: (layer, 0, 0)), vec(), vec(), vec()],
        out_specs=pl.BlockSpec((ts, cw), lambda i: (i, 0)),
        scratch_shapes=[pltpu.VMEM((ts + 2 * CONF_HALO, cw), F32),
                        pltpu.VMEM((SUBLANES, ts + 2 * CONF_HALO - SUBLANES, cw), F32)],
        compiler_params=_params("parallel"),
        name="conformer_conv",
    )(conv_in, conv_in, conv_in, conv_in, conv_in, conv_in, dw_w,
      dw_b.reshape(depth, 1, cw), ln_g.reshape(depth, 1, cw), ln_b.reshape(depth, 1, cw))


def _hy_short_kernel(p_ref, c_ref, n_ref, w_ref, b_ref, o_ref, ybuf, *, ts, n_lat, tps_lat, tps_ctx):
    first, last = _seq_edges(pl.program_id(0), n_lat, tps_lat, tps_ctx)
    halo = HY_HALO
    prev = p_ref[...]
    nxt = n_ref[...]
    ybuf[0:halo, :] = jnp.where(first, jnp.zeros_like(prev), prev)
    ybuf[halo:halo + ts, :] = c_ref[...]
    ybuf[halo + ts:, :] = jnp.where(last, jnp.zeros_like(nxt), nxt)
    acc = b_ref[...] + w_ref[0:1, :] * ybuf[halo - 1:halo - 1 + ts, :]
    acc = acc + w_ref[1:2, :] * ybuf[halo:halo + ts, :]
    acc = acc + w_ref[2:3, :] * ybuf[halo + 1:halo + 1 + ts, :]
    o_ref[...] = acc


def _hyena_short(conv_in, sc_w, sc_b, layer, dims, n_rows, ts):
    cw, hw = dims["CW"], dims["HW"]
    depth = sc_w.shape[0]
    n_lat = dims["T"] // ts
    col0 = 2 * cw // hw
    specs = _halo_specs(ts, HY_HALO, hw, lambda m: col0 + m, conv_in.shape[0])
    kern = functools.partial(_hy_short_kernel, ts=ts, n_lat=n_lat, tps_lat=dims["S"] // ts,
                             tps_ctx=dims["Lc"] // ts)
    return pl.pallas_call(
        kern,
        out_shape=jax.ShapeDtypeStruct((n_rows, 3 * hw), F32),
        grid=(n_rows // ts, 3),
        in_specs=[*specs,
                  pl.BlockSpec((None, HY_SHORT_K, hw), lambda i, m: (layer, 0, m)),
                  pl.BlockSpec((None, 1, hw), lambda i, m: (layer, 0, m))],
        out_specs=pl.BlockSpec((ts, hw), lambda i, m: (i, m)),
        scratch_shapes=[pltpu.VMEM((ts + 2 * HY_HALO, hw), F32)],
        compiler_params=_params("parallel", "arbitrary"),
        name="hyena_short_conv",
    )(conv_in, conv_in, conv_in, sc_w, sc_b.reshape(depth, 1, 3 * hw))


def _filter_kernel(emb_ref, w1_ref, b1_ref, f_ref, w2_ref, b2_ref, w3_ref, dec_ref, hs_ref, hd_ref, asum_ref,
                   altb_ref, *, hw, tl):
    i = pl.program_id(0)
    hp = lax.Precision.HIGHEST
    emb = emb_ref[...]
    t = emb[:, 0:1]
    h1 = jnp.sin(f_ref[0:1, :] * (jnp.dot(emb, w1_ref[...], precision=hp, preferred_element_type=F32) + b1_ref[...]))
    h2 = jnp.sin(f_ref[1:2, :] * (jnp.dot(h1, w2_ref[...], precision=hp, preferred_element_type=F32) + b2_ref[...]))
    h = jnp.dot(h2, w3_ref[...], precision=hp, preferred_element_type=F32)
    h = h * jnp.exp(-t * jnp.abs(dec_ref[...]))
    col = lax.broadcasted_iota(jnp.int32, h.shape, 1)
    row = lax.broadcasted_iota(jnp.int32, h.shape, 0) + i * tl
    backward = (col // hw) % 2 == 1
    h = jnp.where(jnp.logical_and(backward, row == 0), 0.0, h)

    @pl.when(i == 0)
    def _():
        asum_ref[...] = jnp.zeros_like(asum_ref)
        altb_ref[...] = jnp.zeros_like(altb_ref)

    def fold(v):
        return v.reshape(tl // SUBLANES, SUBLANES, v.shape[1]).sum(axis=0)

    asum_ref[...] += fold(jnp.abs(h))
    alt = (1 - 2 * ((lax.broadcasted_iota(jnp.int32, (tl, hw), 0) + i * tl) & 1)).astype(F32)
    for o in range(h.shape[1] // (2 * hw)):
        hf = h[:, 2 * o * hw:(2 * o + 1) * hw]
        hb = h[:, (2 * o + 1) * hw:(2 * o + 2) * hw]
        sl = slice(o * hw, (o + 1) * hw)
        hs_ref[:, sl] = (hf + hb).astype(hs_ref.dtype)
        hd_ref[:, sl] = (hf - hb).astype(hd_ref.dtype)
        altb_ref[:, sl] += fold(hb * alt)


def _hyena_filter(emb, w1, b1, freq, w2, b2, w3, decay, layer, hw):
    l, ke = emb.shape
    depth, _, ffn = w1.shape
    nf = w3.shape[2]
    tl = _largest_tile((512, 256, 128), l)
    w1p = jnp.pad(w1, ((0, 0), (0, ke - w1.shape[1]), (0, 0)))
    vec = lambda n: pl.BlockSpec((None, 1, n), lambda i: (layer, 0, 0))
    half = jax.ShapeDtypeStruct((l, nf // 2), BF16)
    return pl.pallas_call(
        functools.partial(_filter_kernel, hw=hw, tl=tl),
        out_shape=(half, half, jax.ShapeDtypeStruct((SUBLANES, nf), F32),
                   jax.ShapeDtypeStruct((SUBLANES, nf // 2), F32)),
        grid=(l // tl,),
        in_specs=[
            pl.BlockSpec((tl, ke), lambda i: (i, 0)),
            pl.BlockSpec((None, ke, ffn), lambda i: (layer, 0, 0)),
            vec(ffn),
            pl.BlockSpec((None, 2, ffn), lambda i: (layer, 0, 0)),
            pl.BlockSpec((None, ffn, ffn), lambda i: (layer, 0, 0)),
            vec(ffn),
            pl.BlockSpec((None, ffn, nf), lambda i: (layer, 0, 0)),
            vec(nf),
        ],
        out_specs=(pl.BlockSpec((tl, nf // 2), lambda i: (i, 0)), pl.BlockSpec((tl, nf // 2), lambda i: (i, 0)),
                   pl.BlockSpec((SUBLANES, nf), lambda i: (0, 0)), pl.BlockSpec((SUBLANES, nf // 2), lambda i: (0, 0))),
        compiler_params=_params("arbitrary"),
        name="hyena_filter_mlp",
    )(emb, w1p, b1.reshape(depth, 1, ffn), freq, w2, b2.reshape(depth, 1, ffn), w3, decay.reshape(depth, 1, nf))


def _bin_scale(i, tm, n_bins):
    row = lax.broadcasted_iota(jnp.int32, (tm, 1), 0) + i * tm
    is0 = row == 0
    return is0, jnp.where(is0, 1.0 / (2 * n_bins), 2.0 / (2 * n_bins)).astype(F32)


def _filter_dft_kernel(c_ref, s_ref, hs_ref, hd_ref, af_ref, ab_ref, altb_ref, kre_ref, kim_ref, *, tm, n_bins):
    i = pl.program_id(1)
    re = _dot(c_ref[...], hs_ref[...])
    im = _dot(s_ref[...], hd_ref[...])
    norm = jnp.sum(af_ref[...], axis=0, keepdims=True) + jnp.sum(ab_ref[...], axis=0, keepdims=True)
    nyq_b = jnp.sum(altb_ref[...], axis=0, keepdims=True)
    is0, cs = _bin_scale(i, tm, n_bins)
    w = cs / norm
    kre_ref[...] = re * w
    kim_ref[...] = jnp.where(is0, im + 2.0 * nyq_b, im) * w


def _filter_spectrum(hs, hd, asum, altb, c_tab, s_tab, hw):
    l, n = hs.shape
    n_ord = n // hw
    tm = _largest_tile((512, 256, 128), l)
    out = jax.ShapeDtypeStruct((l, n_ord * hw), F32)
    tab = pl.BlockSpec((tm, l), lambda o, i: (i, 0))
    col = pl.BlockSpec((l, hw), lambda o, i: (0, o))
    return pl.pallas_call(
        functools.partial(_filter_dft_kernel, tm=tm, n_bins=l),
        out_shape=(out, out),
        grid=(n_ord, l // tm),
        in_specs=[tab, tab, col, col,
                  pl.BlockSpec((SUBLANES, hw), lambda o, i: (0, 2 * o)),
                  pl.BlockSpec((SUBLANES, hw), lambda o, i: (0, 2 * o + 1)),
                  pl.BlockSpec((SUBLANES, hw), lambda o, i: (0, o))],
        out_specs=(pl.BlockSpec((tm, hw), lambda o, i: (i, o)), pl.BlockSpec((tm, hw), lambda o, i: (i, o))),
        compiler_params=_params("parallel", "arbitrary"),
        name="hyena_filter_dft",
    )(c_tab, s_tab, hs, hd, asum, asum, altb)


def _fwd_dft_kernel(c_ref, s_ref, z_ref, kre_ref, kim_ref, yre_ref, yim_ref, zb, *, tm):
    i = pl.program_id(1)

    @pl.when(i == 0)
    def _():
        zb[...] = z_ref[...].astype(BF16)

    xre = _dot(c_ref[...], zb[...])
    xim = _dot(s_ref[...], zb[...])
    kre, kim = kre_ref[...], kim_ref[...]
    is0 = (lax.broadcasted_iota(jnp.int32, (tm, 1), 0) + i * tm) == 0
    yre = xre * kre - jnp.where(is0, 0.0, xim * kim)
    yim = jnp.where(is0, xim * kim, xre * kim + xim * kre)
    yre_ref[...] = yre.astype(BF16)
    yim_ref[...] = yim.astype(BF16)


def _fwd_dft(z_arr, z_row0, z_col, kre, kim, order, c_tab, s_tab, b, l, hw):
    tm = _largest_tile((512, 256, 128), l)
    out = jax.ShapeDtypeStruct((b * l, hw), BF16)
    tab = pl.BlockSpec((tm, l), lambda bi, i: (i, 0))
    kspec = pl.BlockSpec((tm, hw), lambda bi, i: (i, order))
    ospec = pl.BlockSpec((tm, hw), lambda bi, i: (bi * (l // tm) + i, 0))
    return pl.pallas_call(
        functools.partial(_fwd_dft_kernel, tm=tm),
        out_shape=(out, out),
        grid=(b, l // tm),
        in_specs=[tab, tab, pl.BlockSpec((l, hw), lambda bi, i: (z_row0 // l + bi, z_col)), kspec, kspec],
        out_specs=(ospec, ospec),
        scratch_shapes=[pltpu.VMEM((l, hw), BF16)],
        compiler_params=_params("parallel", "arbitrary"),
        name="hyena_fwd_dft",
    )(c_tab, s_tab, z_arr, kre, kim)


def _inv_dft_kernel(c_ref, st_ref, yre_ref, yim_ref, gate_ref, z_ref, bias_ref, o_ref):
    y = _dot(c_ref[...], yre_ref[...]) + _dot(st_ref[...], yim_ref[...])
    o_ref[...] = (gate_ref[...] * (y + bias_ref[...] * z_ref[...])).astype(o_ref.dtype)


def _inv_dft(yre, yim, u, u_row0, gate_col, z_arr, z_row0, z_col, bias, layer, order, c_tab, st_tab, b, l, hw,
             out_dtype):
    tm = _largest_tile((512, 256, 128), l)
    per = l // tm
    tab = pl.BlockSpec((tm, l), lambda bi, i: (i, 0))
    yspec = pl.BlockSpec((l, hw), lambda bi, i: (bi, 0))
    return pl.pallas_call(
        _inv_dft_kernel,
        out_shape=jax.ShapeDtypeStruct((b * l, hw), out_dtype),
        grid=(b, per),
        in_specs=[tab, tab, yspec, yspec,
                  pl.BlockSpec((tm, hw), lambda bi, i: (u_row0 // tm + bi * per + i, gate_col)),
                  pl.BlockSpec((tm, hw), lambda bi, i: (z_row0 // tm + bi * per + i, z_col)),
                  pl.BlockSpec((None, None, 1, hw), lambda bi, i: (layer, order, 0, 0))],
        out_specs=pl.BlockSpec((tm, hw), lambda bi, i: (bi * per + i, 0)),
        compiler_params=_params("parallel", "arbitrary"),
        name="hyena_inv_dft",
    )(c_tab, st_tab, yre, yim, u, z_arr, bias.reshape(bias.shape[0], bias.shape[1], 1, hw))


def _dft_table_kernel(c_ref, s_ref, st_ref, *, tm, l):
    n = 2 * l
    w = 2.0 * math.pi / n
    row = lax.broadcasted_iota(jnp.int32, (tm, LANES), 0) + pl.program_id(0) * tm
    lane = lax.broadcasted_iota(jnp.int32, (tm, LANES), 1)
    fine = ((row * lane) & (n - 1)).astype(F32) * w
    coarse = ((row * (lane * LANES)) & (n - 1)).astype(F32) * w
    cb, sb = jnp.cos(fine), jnp.sin(fine)
    ca, sa = jnp.cos(coarse), jnp.sin(coarse)
    alt_row = (1 - 2 * (row & 1)).astype(F32)
    for j in range(l // LANES):
        caj = jnp.broadcast_to(ca[:, j:j + 1], (tm, LANES))
        saj = jnp.broadcast_to(sa[:, j:j + 1], (tm, LANES))
        col = lane + j * LANES
        cos_blk = caj * cb - saj * sb
        nsin_blk = -(saj * cb + caj * sb)
        alt_col = (1 - 2 * (col & 1)).astype(F32)
        sl = slice(j * LANES, (j + 1) * LANES)
        c_ref[:, sl] = cos_blk.astype(BF16)
        s_ref[:, sl] = jnp.where(row == 0, alt_col, nsin_blk).astype(BF16)
        st_ref[:, sl] = jnp.where(col == 0, alt_row, nsin_blk).astype(BF16)


def _dft_tables(l):
    assert l % LANES == 0 and (l & (l - 1)) == 0 and l // LANES <= LANES
    tm = _largest_tile((256, 128), l)
    out = jax.ShapeDtypeStruct((l, l), BF16)
    spec = pl.BlockSpec((tm, l), lambda i: (i, 0))
    return pl.pallas_call(
        functools.partial(_dft_table_kernel, tm=tm, l=l),
        out_shape=(out, out, out),
        grid=(l // tm,),
        out_specs=(spec, spec, spec),
        compiler_params=_params("parallel"),
        name="dft_tables",
    )()


def _hyena_embedding(l):
    n = jnp.arange(l, dtype=F32)
    t = n / max(l - 1, 1)
    w = 2.0 * math.pi * n / l
    f = jnp.linspace(1e-4, HY_BANDS - 1, HY_BANDS, dtype=F32)
    fw = w[:, None] * f[None, :]
    emb = jnp.concatenate([t[:, None], jnp.cos(fw), -jnp.sin(fw)], axis=-1)
    return jnp.pad(emb, ((0, 0), (0, LANES - HY_EMB)))


def _hyena_long(u, u_row0, b, l, layer, p, tables, emb, dims):
    hw = dims["HW"]
    c_tab, s_tab, st_tab = tables
    hs, hd, asum, altb = _hyena_filter(emb, p["hy_w1"], p["hy_b1"], p["hy_freq"], p["hy_w2"], p["hy_b2"],
                                       p["hy_w3"], p["hy_decay"], layer, hw)
    kre, kim = _filter_spectrum(hs, hd, asum, altb, c_tab, s_tab, hw)
    yre, yim = _fwd_dft(u, u_row0, 2, kre, kim, 0, c_tab, s_tab, b, l, hw)
    z1 = _inv_dft(yre, yim, u, u_row0, 0, u, u_row0, 2, p["hy_bias"], layer, 0, c_tab, st_tab, b, l, hw, F32)
    yre, yim = _fwd_dft(z1, 0, 0, kre, kim, 1, c_tab, s_tab, b, l, hw)
    return _inv_dft(yre, yim, u, u_row0, 1, z1, 0, 0, p["hy_bias"], layer, 1, c_tab, st_tab, b, l, hw, BF16)


def _merge_kernel(attl_ref, attc_ref, conf_ref, hyl_ref, hyc_ref, wa_ref, wc_ref, wh_ref, g0_ref, g1_ref, g2_ref,
                  o_ref, *, n_lat):
    is_lat = pl.program_id(0) < n_lat
    a = _dot(jnp.where(is_lat, attl_ref[...], attc_ref[...]), wa_ref[...])
    c = _dot(conf_ref[...], wc_ref[...])
    h = _dot(jnp.where(is_lat, hyl_ref[...], hyc_ref[...]), wh_ref[...])
    m = g0_ref[...].astype(F32) * a + g1_ref[...].astype(F32) * c + g2_ref[...].astype(F32) * h
    o_ref[...] = m.astype(o_ref.dtype)


def _merge(att, att_ctx, conf, hy, hy_ctx, gates, w_attn_o, w_conf_o, w_hy_o, layer, n_rows, n_lat_rows, d, tm, tn):
    aw, cw, hw = att.shape[1], conf.shape[1], hy.shape[1]
    gb = d // tn
    n_lat = n_lat_rows // tm
    lat_map = lambda i, j: (jnp.minimum(i, n_lat - 1), 0)
    ctx_map = lambda i, j: (jnp.maximum(i - n_lat, 0), 0)
    return pl.pallas_call(
        functools.partial(_merge_kernel, n_lat=n_lat),
        out_shape=jax.ShapeDtypeStruct((n_rows, d), BF16),
        grid=(n_rows // tm, d // tn),
        in_specs=[
            pl.BlockSpec((tm, aw), lat_map),
            pl.BlockSpec((tm, aw), ctx_map),
            pl.BlockSpec((tm, cw), lambda i, j: (i, 0)),
            pl.BlockSpec((tm, hw), lat_map),
            pl.BlockSpec((tm, hw), ctx_map),
            pl.BlockSpec((aw, tn), lambda i, j: (0, j)),
            pl.BlockSpec((cw, tn), lambda i, j: (0, j)),
            pl.BlockSpec((hw, tn), lambda i, j: (0, j)),
            pl.BlockSpec((tm, tn), lambda i, j: (i, j)),
            pl.BlockSpec((tm, tn), lambda i, j: (i, gb + j)),
            pl.BlockSpec((tm, tn), lambda i, j: (i, 2 * gb + j)),
        ],
        out_specs=pl.BlockSpec((tm, tn), lambda i, j: (i, j)),
        compiler_params=_params("parallel", "arbitrary"),
        name="branch_proj_merge",
    )(att, att_ctx, conf, hy, hy_ctx, w_attn_o, w_conf_o, w_hy_o, gates, gates, gates)


def _out_proj_kernel(a_ref, w_ref, xa_ref, xb_ref, g_ref, o_ref, *, n_first):
    x = jnp.where(pl.program_id(0) < n_first, xa_ref[...], xb_ref[...])
    o_ref[...] = x + g_ref[...] * _dot(a_ref[...], w_ref[...])


def _out_proj(merged, w_out_bf16, x_pair, mod3, layer, n_rows, dims, tm, tn):
    xa, xb = x_pair
    d = xa.shape[1]
    n_lat = dims["T"] // tm
    n_first = xa.shape[0] // tm
    mm = functools.partial(_mod_row_map, layer, tm, n_lat, dims["S"], dims["B"])

    def gate_map(i, j):
        row, z, _ = mm(0)(i)
        return (row, z, 2 * (d // tn) + j)

    return pl.pallas_call(
        functools.partial(_out_proj_kernel, n_first=n_first),
        out_shape=jax.ShapeDtypeStruct((n_rows, d), F32),
        grid=(n_rows // tm, d // tn),
        in_specs=[
            pl.BlockSpec((tm, d), lambda i, j: (i, 0)),
            pl.BlockSpec((d, tn), lambda i, j: (0, j)),
            pl.BlockSpec((tm, tn), lambda i, j: (jnp.minimum(i, n_first - 1), j)),
            pl.BlockSpec((tm, tn), lambda i, j: (jnp.maximum(i - n_first, 0), j)),
            pl.BlockSpec((None, 1, tn), gate_map),
        ],
        out_specs=pl.BlockSpec((tm, tn), lambda i, j: (i, j)),
        compiler_params=_params("parallel", "arbitrary"),
        name="out_proj_residual",
    )(merged, w_out_bf16, xa, xb, mod3)


def _cast_kernel(w_ref, o_ref):
    o_ref[...] = w_ref[...].astype(o_ref.dtype)


def _layer_weight_bf16(w, layer):
    _, k, n = w.shape
    tk = _largest_tile((512, 256, 128), k)
    return pl.pallas_call(
        _cast_kernel,
        out_shape=jax.ShapeDtypeStruct((k, n), BF16),
        grid=(k // tk,),
        in_specs=[pl.BlockSpec((None, tk, n), lambda i: (layer, i, 0))],
        out_specs=pl.BlockSpec((tk, n), lambda i: (i, 0)),
        compiler_params=_params("parallel"),
        name="weight_to_bf16",
    )(w)


def _store_token_tiled(ref, x):
    n, d = x.shape
    ns = d // LANES
    for s in range(ns):
        ref[pl.ds(s, n, stride=ns), :] = x[:, s * LANES:(s + 1) * LANES]


def _router_kernel(x_ref, g_ref, sh_ref, sc_ref, wr_ref, h_ref, r_ref):
    h = _ada_norm_value(x_ref[...], g_ref[...], sh_ref[...], sc_ref[...])
    _store_token_tiled(h_ref, h)
    logits = _dot3(h, wr_ref[...])
    lane = lax.broadcasted_iota(jnp.int32, logits.shape, 1).astype(F32)
    ninf = jnp.float32(-jnp.inf)
    big = jnp.float32(LANES)

    def first_argmax(v):
        mx = jnp.max(v, axis=-1, keepdims=True)
        idx = jnp.min(jnp.where(v == mx, lane, big), axis=-1, keepdims=True)
        return mx, idx

    lg = jnp.where(lane < N_GROUPS, logits, ninf)
    gmax, gidx = first_argmax(lg)
    p_grp = 1.0 / jnp.sum(jnp.exp(lg - gmax), axis=-1, keepdims=True)
    lo = N_GROUPS + gidx * E_PER_GROUP
    le = jnp.where(jnp.logical_and(lane >= lo, lane < lo + E_PER_GROUP), logits, ninf)
    e1, i1 = first_argmax(le)
    e2, i2 = first_argmax(jnp.where(lane == i1, ninf, le))
    t = jnp.exp(e2 - e1)
    w1 = p_grp / (1.0 + t)
    w2 = p_grp * t / (1.0 + t)
    out = jnp.where(lane == 0, i1 - N_GROUPS,
                    jnp.where(lane == 1, i2 - N_GROUPS,
                              jnp.where(lane == 2, w1, jnp.where(lane == 3, w2, 0.0))))
    r_ref[...] = out


def _router(x_all, norm_g, mod3, w_router, layer, n_rows, dims, tm):
    d = x_all.shape[1]
    ns = d // LANES
    n_lat = dims["T"] // tm
    mm = functools.partial(_mod_row_map, layer, tm, n_lat, dims["S"], dims["B"])
    return pl.pallas_call(
        _router_kernel,
        out_shape=(jax.ShapeDtypeStruct((n_rows * ns, LANES), F32), jax.ShapeDtypeStruct((n_rows, LANES), F32)),
        grid=(n_rows // tm,),
        in_specs=[
            pl.BlockSpec((tm, d), lambda i: (i, 0)),
            pl.BlockSpec((None, 1, d), lambda i: (layer, 0, 0)),
            pl.BlockSpec((None, 1, d), mm(3)),
            pl.BlockSpec((None, 1, d), mm(4)),
            pl.BlockSpec((d, LANES), lambda i: (0, 0)),
        ],
        out_specs=(pl.BlockSpec((tm * ns, LANES), lambda i: (i, 0)), pl.BlockSpec((tm, LANES), lambda i: (i, 0))),
        compiler_params=_params("parallel"),
        name="adanorm2_router",
    )(x_all, norm_g.reshape(norm_g.shape[0], 1, d), mod3, mod3, w_router)


def _dispatch(route, n_tok, bm):
    eid = route[:, 0:2].astype(jnp.int32).reshape(-1)
    a = 2 * n_tok
    e_s, order = lax.sort_key_val(eid, jnp.arange(a, dtype=jnp.int32))
    experts = jnp.arange(N_EXPERTS, dtype=jnp.int32)
    cstart = jnp.searchsorted(e_s, experts, side='left').astype(jnp.int32)
    counts = jnp.searchsorted(e_s, experts, side='right').astype(jnp.int32) - cstart
    pcounts = (counts + bm - 1) // bm * bm
    pend = jnp.cumsum(pcounts)
    pstart = pend - pcounts
    n_blocks = -(-a // bm) + N_EXPERTS
    blk = jnp.arange(n_blocks, dtype=jnp.int32)
    blk_e = jnp.minimum(jnp.searchsorted(pend, blk * bm, side='right'), N_EXPERTS - 1).astype(jnp.int32)
    k = blk - pstart[blk_e] // bm
    blk_src0 = jnp.clip(cstart[blk_e] + k * bm, 0, a - 1).astype(jnp.int32)
    blk_nvalid = jnp.clip(counts[blk_e] - k * bm, 0, bm).astype(jnp.int32)
    n_used = (pend[-1] // bm).astype(jnp.int32).reshape(1)
    return order, blk_e, blk_src0, blk_nvalid, n_used, n_blocks


def _token_copy(src, src_tok, dst, dst_tok, ns, sem):
    return pltpu.make_async_copy(src.at[pl.ds(pl.multiple_of(src_tok * ns, ns), ns), :],
                                 dst.at[pl.ds(pl.multiple_of(dst_tok * ns, ns), ns), :], sem)


def _expert_kernel(tok_ref, order_ref, be_ref, src0_ref, nvalid_ref, nused_ref, h_hbm, wg_ref, wu_ref, wd_ref,
                   y_hbm, wg_s, wu_s, wd_s, xbuf, xb, obuf, gsem, ssem, *, bm, ns, n_assign, n_blocks):
    i = pl.program_id(0)
    slot = i % 2
    n_used = nused_ref[0]
    changed = jnp.logical_or(i == 0, be_ref[i] != be_ref[jnp.maximum(i - 1, 0)])

    def gather(block, sl):
        base = src0_ref[block]

        def issue(r, carry):
            _token_copy(h_hbm, tok_ref[base + r], xbuf.at[sl], r, ns, gsem.at[sl]).start()
            return carry
        lax.fori_loop(0, bm, issue, 0, unroll=True)

    def wait_all(buf, hbm, sem, sl):
        pltpu.make_async_copy(buf.at[sl], hbm.at[pl.ds(0, bm * ns), :], sem.at[sl]).wait()

    @pl.when(i == 0)
    def _():
        gather(0, 0)

    @pl.when(i >= 2)
    def _():
        wait_all(obuf, y_hbm, ssem, slot)

    @pl.when(i <= n_used)
    def _():
        wait_all(xbuf, h_hbm, gsem, slot)

    @pl.when(i < n_used)
    def _():
        for s in range(ns):
            xb[:, s * LANES:(s + 1) * LANES] = xbuf[slot, pl.ds(s, bm, stride=ns), :].astype(BF16)

        @pl.when(changed)
        def _():
            wg_s[...] = wg_ref[...].astype(BF16)
            wu_s[...] = wu_ref[...].astype(BF16)
            wd_s[...] = wd_ref[...].astype(BF16)

        gather(i + 1, 1 - slot)
        x = xb[...]
        g = _dot(x, wg_s[...])
        u = _dot(x, wu_s[...])
        mid = (g * jax.nn.sigmoid(g) * u).astype(BF16)
        _store_token_tiled(obuf.at[slot], _dot(mid, wd_s[...]))

    @pl.when(i >= n_used)
    def _():
        obuf[slot] = jnp.zeros(obuf.shape[1:], obuf.dtype)

    base = src0_ref[i]
    n_real = nvalid_ref[i]

    def issue_out(r, carry):
        dst = jnp.where(r < n_real, order_ref[base + r], n_assign + slot * bm + r)
        _token_copy(obuf.at[slot], r, y_hbm, dst, ns, ssem.at[slot]).start()
        return carry
    lax.fori_loop(0, bm, issue_out, 0, unroll=True)

    @pl.when(i == n_blocks - 1)
    def _():
        if n_blocks > 1:
            wait_all(obuf, y_hbm, ssem, 1 - slot)
        wait_all(obuf, y_hbm, ssem, slot)

        @pl.when(i < n_used)
        def _():
            wait_all(xbuf, h_hbm, gsem, 1 - slot)


def _experts(h2t, order, blk_e, blk_src0, blk_nvalid, n_used, w_gate, w_up, w_down, layer, n_blocks, bm, d):
    f = w_gate.shape[3]
    ns = d // LANES
    n_assign = order.shape[0]
    order_p = jnp.concatenate([order, jnp.zeros((bm,), jnp.int32)])
    tok_p = order_p // 2
    blk_src0 = jnp.concatenate([blk_src0, jnp.zeros((1,), jnp.int32)])
    wspec = lambda shape: pl.BlockSpec((None, None) + shape, lambda i, t, o, be, *_: (layer, be[i], 0, 0))
    return pl.pallas_call(
        functools.partial(_expert_kernel, bm=bm, ns=ns, n_assign=n_assign, n_blocks=n_blocks),
        out_shape=jax.ShapeDtypeStruct(((n_assign + 2 * bm) * ns, LANES), F32),
        grid_spec=pltpu.PrefetchScalarGridSpec(
            num_scalar_prefetch=6,
            grid=(n_blocks,),
            in_specs=[pl.BlockSpec(memory_space=pl.ANY), wspec((d, f)), wspec((d, f)), wspec((f, d))],
            out_specs=pl.BlockSpec(memory_space=pl.ANY),
            scratch_shapes=[pltpu.VMEM((d, f), BF16), pltpu.VMEM((d, f), BF16), pltpu.VMEM((f, d), BF16),
                            pltpu.VMEM((2, bm * ns, LANES), F32), pltpu.VMEM((bm, d), BF16),
                            pltpu.VMEM((2, bm * ns, LANES), F32),
                            pltpu.SemaphoreType.DMA((2,)), pltpu.SemaphoreType.DMA((2,))],
        ),
        compiler_params=_params("arbitrary"),
        name="moe_experts",
    )(tok_p, order_p, blk_e, blk_src0, blk_nvalid, n_used, h2t, w_gate, w_up, w_down)


def _combine_kernel(y_ref, x_ref, g_ref, route_ref, *rest, tc, ns, final):
    if final:
        ng_ref, o_ref = rest
    else:
        (o_ref,) = rest
    w0 = route_ref[:, 2:3]
    w1 = route_ref[:, 3:4]
    ssq = jnp.zeros((tc, 1), F32)
    for s in range(ns):
        sl = slice(s * LANES, (s + 1) * LANES)
        y = w0 * y_ref[pl.ds(s, tc, stride=2 * ns), :] + w1 * y_ref[pl.ds(ns + s, tc, stride=2 * ns), :]
        xn = x_ref[:, sl] + g_ref[:, sl] * y
        if final:
            ssq = ssq + jnp.sum(xn * xn, axis=-1, keepdims=True)
        o_ref[:, sl] = xn
    if final:
        o_ref[...] = o_ref[...] * lax.rsqrt(ssq / (ns * LANES) + EPS) * ng_ref[...]


def _combine(ys, route, x_all, mod3, layer, n_rows, dims, tc, norm_f_g=None):
    d = x_all.shape[1]
    ns = d // LANES
    n_lat = dims["T"] // tc
    mm = functools.partial(_mod_row_map, layer, tc, n_lat, dims["S"], dims["B"])
    final = norm_f_g is not None
    in_specs = [
        pl.BlockSpec((tc * 2 * ns, LANES), lambda i: (i, 0)),
        pl.BlockSpec((tc, d), lambda i: (i, 0)),
        pl.BlockSpec((None, 1, d), mm(5)),
        pl.BlockSpec((tc, LANES), lambda i: (i, 0)),
    ]
    args = [ys, x_all, mod3, route]
    if final:
        in_specs.append(pl.BlockSpec((1, d), lambda i: (0, 0)))
        args.append(norm_f_g.reshape(1, d))
    return pl.pallas_call(
        functools.partial(_combine_kernel, tc=tc, ns=ns, final=final),
        out_shape=jax.ShapeDtypeStruct((n_rows, d), F32),
        grid=(n_rows // tc,),
        in_specs=in_specs,
        out_specs=pl.BlockSpec((tc, d), lambda i: (i, 0)),
        compiler_params=_params("parallel"),
        name="moe_combine_residual",
    )(*args)


def _rope_table(s):
    half = ATT_DQK // 2
    rows = jnp.repeat(jnp.arange(s // GRID_W, dtype=jnp.int32), GRID_W)
    cols = jnp.tile(jnp.arange(GRID_W, dtype=jnp.int32), s // GRID_W)
    inv = ROPE_BASE ** (-jnp.arange(0, half, 2, dtype=F32) / half)
    ar = rows.astype(F32)[:, None] * inv
    ac = cols.astype(F32)[:, None] * inv
    cr, sr, cc, sc = jnp.cos(ar), jnp.sin(ar), jnp.cos(ac), jnp.sin(ac)
    z = jnp.zeros_like(sr)
    c = jnp.concatenate([cr, cr, cc, cc] * 2, axis=-1)
    s1 = jnp.concatenate([-sr, z, -sc, z] * 2, axis=-1)
    s2 = jnp.concatenate([z, sr, z, sc] * 2, axis=-1)
    return jnp.concatenate([c, s1, s2], axis=-1)


def kernel(x, c, ctx, c_ctx, w_mod, b_mod, norm1_g, norm2_g, w_in, lam_q1, lam_k1, lam_q2, lam_k2, attn_subln_g, w_attn_o, conf_dw_w, conf_dw_b, conf_ln_g, conf_ln_b, w_conf_o, hy_sc_w, hy_sc_b, hy_w1, hy_b1, hy_w2, hy_b2, hy_freq, hy_w3, hy_decay, hy_bias, w_hy_o, w_out, w_router_group, w_router_expert, w_exp_gate, w_exp_up, w_exp_down, norm_f_g):
    b, s, d = x.shape
    lc = ctx.shape[1]
    depth = w_mod.shape[0]
    t, tc_rows = b * s, b * lc
    cw, hw = conf_dw_w.shape[2], w_hy_o.shape[1]
    dims = dict(B=b, S=s, Lc=lc, T=t, QK=ATT_HEADS * 2 * ATT_DQK, AW=ATT_HEADS * ATT_DV, CW=cw, HW=hw)
    assert b + 1 <= MOD_ROWS and 2 * ATT_DQK == LANES and ATT_DV == LANES and hw == cw
    tm = _largest_tile((1024, 512, 256, 128), s, tc_rows)
    ts = _largest_tile((256, 128), s, lc)
    tn = _largest_tile((512, 256, 128), dims["QK"], dims["AW"], 2 * cw, 3 * hw, d)
    tcomb = _largest_tile((256, 128), s, tc_rows)

    x_pair = (x.reshape(t, d), ctx.reshape(tc_rows, d))
    cond = jnp.zeros((MOD_ROWS, d), F32).at[:b].set(c).at[b].set(c_ctx)
    mod3 = _modulation(cond, w_mod, b_mod).reshape(depth * MOD_ROWS, 1, 6 * d)
    rope_tab = _rope_table(s)
    tables_lat, emb_lat = _dft_tables(s), _hyena_embedding(s)
    tables_ctx, emb_ctx = _dft_tables(lc), _hyena_embedding(lc)
    hy_params = dict(hy_w1=hy_w1, hy_b1=hy_b1, hy_w2=hy_w2, hy_b2=hy_b2, hy_freq=hy_freq, hy_w3=hy_w3,
                     hy_decay=hy_decay, hy_bias=hy_bias)

    out = None
    for l in range(depth):
        need_ctx = l < depth - 1
        n_rows = t + tc_rows if need_ctx else t
        lam_init = 0.8 - 0.6 * math.exp(-0.3 * l)
        lam = (jnp.exp(jnp.sum(lam_q1[l] * lam_k1[l])) - jnp.exp(jnp.sum(lam_q2[l] * lam_k2[l])) + lam_init)
        lam = lam.reshape(1).astype(F32)

        hl = _adanorm(x_pair, t + tc_rows, norm1_g, mod3, l, dims, tm)
        qkv, conv_in, gates = _in_proj(hl, w_in, l, rope_tab, dims, tm, tn)
        att, att_ctx = _attention(qkv, lam, attn_subln_g, l, lam_init, dims, need_ctx)
        conf = _conformer(conv_in, conf_dw_w, conf_dw_b, conf_ln_g, conf_ln_b, l, dims, n_rows, ts)
        u = _hyena_short(conv_in, hy_sc_w, hy_sc_b, l, dims, n_rows, ts)
        hy = _hyena_long(u, 0, b, s, l, hy_params, tables_lat, emb_lat, dims)
        hy_ctx = _hyena_long(u, t, b, lc, l, hy_params, tables_ctx, emb_ctx, dims) if need_ctx else hy
        merged = _merge(att, att_ctx, conf, hy, hy_ctx, gates, _layer_weight_bf16(w_attn_o, l),
                        _layer_weight_bf16(w_conf_o, l), _layer_weight_bf16(w_hy_o, l), l, n_rows, t, d, tm, tn)
        x_all = _out_proj(merged, _layer_weight_bf16(w_out, l), x_pair, mod3, l, n_rows, dims, tm, tn)

        w_router = jnp.concatenate(
            [w_router_group[l], jnp.transpose(w_router_expert[l], (1, 0, 2)).reshape(d, N_EXPERTS)], axis=1)
        w_router = jnp.pad(w_router, ((0, 0), (0, LANES - w_router.shape[1])))
        h2t, route = _router(x_all, norm2_g, mod3, w_router, l, n_rows, dims, tm)
        order, blk_e, blk_src0, blk_nvalid, n_used, n_blocks = _dispatch(route, n_rows, MOE_BLOCK)
        ys = _experts(h2t, order, blk_e, blk_src0, blk_nvalid, n_used, w_exp_gate, w_exp_up, w_exp_down, l,
                      n_blocks, MOE_BLOCK, d)
        x_new = _combine(ys, route, x_all, mod3, l, n_rows, dims, tcomb, None if need_ctx else norm_f_g)
        if need_ctx:
            x_pair = (x_new, x_new)
        else:
            out = x_new
    return out.reshape(b, s, d)
```

```python
import functools
import math

import jax
import jax.numpy as jnp
from jax import lax
from jax.experimental import pallas as pl
from jax.experimental.pallas import tpu as pltpu

F32 = jnp.float32
BF16 = jnp.bfloat16

GRID_W = 64
EPS = 1e-6
ATT_HEADS = 8
ATT_DQK = 64
ATT_DV = 2 * ATT_DQK
ROPE_BASE = 10000.0
CONF_K = 31
HY_ORDER = 2
HY_SHORT_K = 3
HY_EMB = 33
HY_BANDS = (HY_EMB - 1) // 2
N_BRANCH = 3
N_GROUPS = 8
E_PER_GROUP = 8
N_EXPERTS = N_GROUPS * E_PER_GROUP

LANES = 128
SUBLANES = 8
VMEM_LIMIT_BYTES = 56 * 1024 * 1024

CONF_HALO = 16
HY_HALO = 8
MOE_BLOCK = 256
ATT_UNROLL = 16
ATT_TQ = (2048, 1024, 512, 256, 128)
ATT_TK = (256, 128)
MOD_ROWS = 8


def _params(*sem):
    return pltpu.CompilerParams(dimension_semantics=sem, vmem_limit_bytes=VMEM_LIMIT_BYTES)


def _largest_tile(cands, *dims):
    for c in cands:
        if all(d % c == 0 for d in dims):
            return c
    raise ValueError(f"no tile in {cands} divides {dims}")


def _dot(a, b):
    return jnp.dot(a, b, preferred_element_type=F32)


def _split_bf16(x):
    hi = x.astype(BF16)
    lo = (x - hi.astype(F32)).astype(BF16)
    return hi, lo


def _dot3(a, b):
    ah, al = _split_bf16(a)
    bh, bl = _split_bf16(b)
    return _dot(ah, bh) + _dot(al, bh) + _dot(ah, bl)


def _mod_kernel(c_ref, w_ref, b_ref, o_ref):
    c = c_ref[...]
    s = c * jax.nn.sigmoid(c)
    o_ref[...] = _dot3(s, w_ref[...]) + b_ref[...]


def _modulation(cond, w_mod, b_mod):
    depth, d, n = w_mod.shape
    tn = _largest_tile((512, 256, 128), n)
    return pl.pallas_call(
        _mod_kernel,
        out_shape=jax.ShapeDtypeStruct((depth, MOD_ROWS, n), F32),
        grid=(depth, n // tn),
        in_specs=[
            pl.BlockSpec((MOD_ROWS, d), lambda l, j: (0, 0)),
            pl.BlockSpec((None, d, tn), lambda l, j: (l, 0, j)),
            pl.BlockSpec((None, 1, tn), lambda l, j: (l, 0, j)),
        ],
        out_specs=pl.BlockSpec((None, MOD_ROWS, tn), lambda l, j: (l, 0, j)),
        compiler_params=_params("parallel", "parallel"),
        name="modulation",
    )(cond, w_mod, b_mod.reshape(depth, 1, n))


def _ada_norm_value(x, g, shift, scale):
    ms = jnp.mean(x * x, axis=-1, keepdims=True)
    y = x * lax.rsqrt(ms + EPS) * g
    return y * (1.0 + scale) + shift


def _adanorm_kernel(xa_ref, xb_ref, g_ref, sh_ref, sc_ref, o_ref, *, n_first):
    x = jnp.where(pl.program_id(0) < n_first, xa_ref[...], xb_ref[...])
    o_ref[...] = _ada_norm_value(x, g_ref[...], sh_ref[...], sc_ref[...]).astype(o_ref.dtype)


def _mod_row_map(layer, tm, n_lat, s, b, col):
    def index_map(i, *_):
        row = jnp.where(i < n_lat, (i * tm) // s, b)
        return (layer * MOD_ROWS + row, 0, col)
    return index_map


def _adanorm(x_pair, n_rows, norm_g, mod3, layer, dims, tm):
    xa, xb = x_pair
    r, d = n_rows, xa.shape[1]
    n_lat = dims["T"] // tm
    n_first = xa.shape[0] // tm
    mm = functools.partial(_mod_row_map, layer, tm, n_lat, dims["S"], dims["B"])
    return pl.pallas_call(
        functools.partial(_adanorm_kernel, n_first=n_first),
        out_shape=jax.ShapeDtypeStruct((r, d), BF16),
        grid=(r // tm,),
        in_specs=[
            pl.BlockSpec((tm, d), lambda i: (jnp.minimum(i, n_first - 1), 0)),
            pl.BlockSpec((tm, d), lambda i: (jnp.maximum(i - n_first, 0), 0)),
            pl.BlockSpec((None, 1, d), lambda i: (layer, 0, 0)),
            pl.BlockSpec((None, 1, d), mm(0)),
            pl.BlockSpec((None, 1, d), mm(1)),
        ],
        out_specs=pl.BlockSpec((tm, d), lambda i: (i, 0)),
        compiler_params=_params("parallel"),
        name="adanorm1",
    )(xa, xb, norm_g.reshape(norm_g.shape[0], 1, d), mod3, mod3)


def _qkv_kernel(a_ref, w_ref, tab_ref, o_ref, *, n_lat, tn, qk_cols):
    i = pl.program_id(0)
    j = pl.program_id(1)
    acc = _dot(a_ref[...], w_ref[...].astype(BF16))
    col0 = j * tn
    scale = jnp.where(col0 < qk_cols, ATT_DQK ** -0.5 * math.log2(math.e), 1.0).astype(F32)
    do_rope = jnp.logical_and(i < n_lat, col0 < 2 * qk_cols)

    @pl.when(do_rope)
    def _():
        c = tab_ref[:, 0:LANES]
        s1 = tab_ref[:, LANES:2 * LANES]
        s2 = tab_ref[:, 2 * LANES:3 * LANES]
        for h in range(tn // LANES):
            xh = acc[:, h * LANES:(h + 1) * LANES] * scale
            rot = xh * c + pltpu.roll(xh, LANES - 16, 1) * s1 + pltpu.roll(xh, 16, 1) * s2
            o_ref[:, h * LANES:(h + 1) * LANES] = rot.astype(o_ref.dtype)

    @pl.when(jnp.logical_not(do_rope))
    def _():
        o_ref[...] = (acc * scale).astype(o_ref.dtype)


def _plain_mm_kernel(a_ref, w_ref, o_ref):
    o_ref[...] = _dot(a_ref[...], w_ref[...].astype(BF16)).astype(o_ref.dtype)


def _sigmoid_mm_kernel(a_ref, w_ref, o_ref):
    w = w_ref[...].astype(BF16)
    half = o_ref.shape[0] // 2
    for c in range(2):
        rows = slice(c * half, (c + 1) * half)
        o_ref[rows, :] = jax.nn.sigmoid(_dot(a_ref[rows, :], w)).astype(o_ref.dtype)


def _in_proj(hl, w_in, layer, rope_tab, dims, tm, tn):
    r, d = hl.shape
    qk, aw, cw, hw = dims["QK"], dims["AW"], dims["CW"], dims["HW"]
    n_lat = dims["T"] // tm
    s_tiles = dims["S"] // tm
    a_spec = pl.BlockSpec((tm, d), lambda i, j: (i, 0))

    def w_spec(col_off):
        off = col_off // tn
        return pl.BlockSpec((None, d, tn), lambda i, j: (layer, 0, j + off))

    def call(kernel, col_off, n_cols, dtype, name, extra_in=(), extra_specs=()):
        return pl.pallas_call(
            kernel,
            out_shape=jax.ShapeDtypeStruct((r, n_cols), dtype),
            grid=(r // tm, n_cols // tn),
            in_specs=[a_spec, w_spec(col_off), *extra_specs],
            out_specs=pl.BlockSpec((tm, tn), lambda i, j: (i, j)),
            compiler_params=_params("parallel", "arbitrary"),
            name=name,
        )(hl, w_in, *extra_in)

    qkv = call(
        functools.partial(_qkv_kernel, n_lat=n_lat, tn=tn, qk_cols=qk), 0, 2 * qk + aw, BF16, "in_proj_qkv",
        extra_in=(rope_tab,),
        extra_specs=(pl.BlockSpec((tm, 3 * LANES), lambda i, j: (i % s_tiles, 0)),))
    conv_in = call(_plain_mm_kernel, 2 * qk + aw, 2 * cw + 3 * hw, F32, "in_proj_conv")
    gates = call(_sigmoid_mm_kernel, 2 * qk + aw + 2 * cw + 3 * hw, N_BRANCH * d, BF16, "in_proj_gates")
    return qkv, conv_in, gates


def _attn_kernel(lam_ref, q_ref, g_ref, kc_ref, vc_ref, *rest, n_lat_chunks, tk, out_scale, unroll):
    if n_lat_chunks:
        kl_ref, vl_ref, o_ref, vce, vle = rest
    else:
        o_ref, vce = rest

    @pl.when(pl.program_id(2) == 0)
    def _():
        vce[:, 0:ATT_DV] = vc_ref[...]
        vce[:, ATT_DV:] = jnp.ones((vce.shape[0], ATT_DV), BF16)
        if n_lat_chunks:
            vle[:, 0:ATT_DV] = vl_ref[...]
            vle[:, ATT_DV:] = jnp.ones((vle.shape[0], ATT_DV), BF16)

    q = q_ref[...]
    lane = lax.broadcasted_iota(jnp.int32, q.shape, 1)
    zero = jnp.zeros_like(q)
    q1 = jnp.where(lane < ATT_DQK, q, zero)
    q2 = jnp.where(lane >= ATT_DQK, q, zero)
    tq = q.shape[0]
    nt = (((1,), (1,)), ((), ()))

    def one_map(qm, k, v, m, acc):
        s = lax.dot_general(qm, k, nt, preferred_element_type=F32)
        m_new = jnp.maximum(m, jnp.max(s, axis=-1, keepdims=True))
        alpha = jnp.exp2(m - m_new)
        p = jnp.exp2((s - m_new).astype(BF16))
        return m_new, alpha * acc + _dot(p, v)

    def step(k, v, carry):
        m1, a1, m2, a2 = carry
        m1, a1 = one_map(q1, k, v, m1, a1)
        m2, a2 = one_map(q2, k, v, m2, a2)
        return m1, a1, m2, a2

    neg = jnp.full((tq, 1), -jnp.inf, F32)
    za = jnp.zeros((tq, 2 * ATT_DV), F32)
    carry = step(kc_ref[...], vce[...], (neg, za, neg, za))
    if n_lat_chunks:
        def body(c, carry):
            off = pl.multiple_of(c * tk, tk)
            return step(kl_ref[pl.ds(off, tk), :], vle[pl.ds(off, tk), :], carry)
        carry = lax.fori_loop(0, n_lat_chunks, body, carry, unroll=unroll)
    _, a1, _, a2 = carry
    o = a1[:, 0:ATT_DV] / a1[:, ATT_DV:] - lam_ref[0] * (a2[:, 0:ATT_DV] / a2[:, ATT_DV:])
    ms = jnp.mean(o * o, axis=-1, keepdims=True)
    o = o * lax.rsqrt(ms + 1e-5) * g_ref[...] * out_scale
    o_ref[...] = o.astype(o_ref.dtype)


def _attention(qkv, lam, subln_g, layer, lam_init, dims, with_ctx):
    b, s, lc, t = dims["B"], dims["S"], dims["Lc"], dims["T"]
    h = ATT_HEADS
    tq = _largest_tile(ATT_TQ, s)
    tk = _largest_tile(ATT_TK, s)
    tqc = _largest_tile((512, 256, 128), lc)
    smem = pl.BlockSpec(memory_space=pltpu.SMEM)
    g3 = subln_g.reshape(subln_g.shape[0], 1, ATT_DV)
    g_spec = pl.BlockSpec((None, 1, ATT_DV), lambda bi, hi, ti: (layer, 0, 0))
    kc_spec = pl.BlockSpec((lc, LANES), lambda bi, hi, ti: (t // lc + bi, h + hi))
    vc_spec = pl.BlockSpec((lc, LANES), lambda bi, hi, ti: (t // lc + bi, 2 * h + hi))
    out_scale = 1.0 - lam_init

    sem = ("arbitrary", "arbitrary", "arbitrary")
    att = pl.pallas_call(
        functools.partial(_attn_kernel, n_lat_chunks=s // tk, tk=tk, out_scale=out_scale,
                          unroll=min(ATT_UNROLL, s // tk)),
        out_shape=jax.ShapeDtypeStruct((t, h * ATT_DV), BF16),
        grid=(b, h, s // tq),
        in_specs=[
            smem,
            pl.BlockSpec((tq, LANES), lambda bi, hi, ti: (bi * (s // tq) + ti, hi)),
            g_spec, kc_spec, vc_spec,
            pl.BlockSpec((s, LANES), lambda bi, hi, ti: (bi, h + hi)),
            pl.BlockSpec((s, LANES), lambda bi, hi, ti: (bi, 2 * h + hi)),
        ],
        out_specs=pl.BlockSpec((tq, LANES), lambda bi, hi, ti: (bi * (s // tq) + ti, hi)),
        scratch_shapes=[pltpu.VMEM((lc, 2 * ATT_DV), BF16), pltpu.VMEM((s, 2 * ATT_DV), BF16)],
        compiler_params=_params(*sem),
        name="diff_attn_latent",
    )(lam, qkv, g3, qkv, qkv, qkv, qkv)
    if not with_ctx:
        return att, att

    att_ctx = pl.pallas_call(
        functools.partial(_attn_kernel, n_lat_chunks=0, tk=tk, out_scale=out_scale, unroll=1),
        out_shape=jax.ShapeDtypeStruct((b * lc, h * ATT_DV), BF16),
        grid=(b, h, lc // tqc),
        in_specs=[
            smem,
            pl.BlockSpec((tqc, LANES), lambda bi, hi, ti: ((t + bi * lc) // tqc + ti, hi)),
            g_spec, kc_spec, vc_spec,
        ],
        out_specs=pl.BlockSpec((tqc, LANES), lambda bi, hi, ti: (bi * (lc // tqc) + ti, hi)),
        scratch_shapes=[pltpu.VMEM((lc, 2 * ATT_DV), BF16)],
        compiler_params=_params(*sem),
        name="diff_attn_ctx",
    )(lam, qkv, g3, qkv, qkv)
    return att, att_ctx


def _seq_edges(i, n_lat, tps_lat, tps_ctx):
    is_lat = i < n_lat
    pos = jnp.where(is_lat, i % tps_lat, (i - n_lat) % tps_ctx)
    n = jnp.where(is_lat, tps_lat, tps_ctx)
    return pos == 0, pos == n - 1


def _halo_specs(ts, halo, width, col, n_rows):
    per = ts // halo
    last = n_rows // halo - 1
    prev = pl.BlockSpec((halo, width), lambda i, *_: (jnp.maximum(i * per - 1, 0), col(*_)))
    cur = pl.BlockSpec((ts, width), lambda i, *_: (i, col(*_)))
    nxt = pl.BlockSpec((halo, width), lambda i, *_: (jnp.minimum((i + 1) * per, last), col(*_)))
    return prev, cur, nxt


def _conf_kernel(ap_ref, ac_ref, an_ref, gp_ref, gc_ref, gn_ref, w_ref, b_ref, lg_ref, lb_ref, o_ref, ybuf, ysh,
                 *, ts, n_lat, tps_lat, tps_ctx, row_chunk):
    first, last = _seq_edges(pl.program_id(0), n_lat, tps_lat, tps_ctx)
    halo = CONF_HALO

    def glu(a, g):
        return a * jax.nn.sigmoid(g)

    prev = glu(ap_ref[...], gp_ref[...])
    nxt = glu(an_ref[...], gn_ref[...])
    ybuf[0:halo, :] = jnp.where(first, jnp.zeros_like(prev), prev)
    ybuf[halo:halo + ts, :] = glu(ac_ref[...], gc_ref[...])
    ybuf[halo + ts:, :] = jnp.where(last, jnp.zeros_like(nxt), nxt)
    n_sh = ysh.shape[1]
    for b in range(1, SUBLANES):
        ysh[b] = ybuf[b:b + n_sh, :]
    base = halo - CONF_K // 2
    for r0 in range(0, ts, row_chunk):
        acc = jnp.zeros((row_chunk, ybuf.shape[1]), F32) + b_ref[...]
        for j in range(CONF_K):
            b = (base + j) % SUBLANES
            start = r0 + base + j - b
            rows = ybuf[start:start + row_chunk, :] if b == 0 else ysh[b, start:start + row_chunk, :]
            acc = acc + w_ref[j:j + 1, :] * rows
        mu = jnp.mean(acc, axis=-1, keepdims=True)
        xc = acc - mu
        var = jnp.mean(xc * xc, axis=-1, keepdims=True)
        y = xc * lax.rsqrt(var + 1e-5) * lg_ref[...] + lb_ref[...]
        o_ref[r0:r0 + row_chunk, :] = (y * jax.nn.sigmoid(y)).astype(o_ref.dtype)


def _conformer(conv_in, dw_w, dw_b, ln_g, ln_b, layer, dims, n_rows, ts):
    cw = dims["CW"]
    depth = dw_w.shape[0]
    n_lat = dims["T"] // ts
    a_specs = _halo_specs(ts, CONF_HALO, cw, lambda: 0, conv_in.shape[0])
    g_specs = _halo_specs(ts, CONF_HALO, cw, lambda: 1, conv_in.shape[0])
    vec = lambda: pl.BlockSpec((None, 1, cw), lambda i: (layer, 0, 0))
    kern = functools.partial(_conf_kernel, ts=ts, n_lat=n_lat, tps_lat=dims["S"] // ts, tps_ctx=dims["Lc"] // ts,
                             row_chunk=min(32, ts))
    return pl.pallas_call(
        kern,
        out_shape=jax.ShapeDtypeStruct((n_rows, cw), BF16),
        grid=(n_rows // ts,),
        in_specs=[*a_specs, *g_specs,
                  pl.BlockSpec((None, CONF_K, cw), lambda i: (layer, 0, 0)), vec(), vec(), vec()],
        out_specs=pl.BlockSpec((ts, cw), lambda i: (i, 0)),
        scratch_shapes=[pltpu.VMEM((ts + 2 * CONF_HALO, cw), F32),
                        pltpu.VMEM((SUBLANES, ts + 2 * CONF_HALO - SUBLANES, cw), F32)],
        compiler_params=_params("parallel"),
        name="conformer_conv",
    )(conv_in, conv_in, conv_in, conv_in, conv_in, conv_in, dw_w,
      dw_b.reshape(depth, 1, cw), ln_g.reshape(depth, 1, cw), ln_b.reshape(depth, 1, cw))


def _hy_short_kernel(*refs, ts, hw, n_lat, tps_lat, tps_ctx):
    w_ref, b_ref, o_ref, ybuf = refs[9:]
    first, last = _seq_edges(pl.program_id(0), n_lat, tps_lat, tps_ctx)
    halo = HY_HALO
    for m in range(3):
        p_ref, c_ref, n_ref = refs[3 * m:3 * m + 3]
        cols = slice(m * hw, (m + 1) * hw)
        prev = p_ref[...]
        nxt = n_ref[...]
        ybuf[0:halo, :] = jnp.where(first, jnp.zeros_like(prev), prev)
        ybuf[halo:halo + ts, :] = c_ref[...]
        ybuf[halo + ts:, :] = jnp.where(last, jnp.zeros_like(nxt), nxt)
        acc = b_ref[:, cols] + w_ref[0:1, cols] * ybuf[halo - 1:halo - 1 + ts, :]
        acc = acc + w_ref[1:2, cols] * ybuf[halo:halo + ts, :]
        acc = acc + w_ref[2:3, cols] * ybuf[halo + 1:halo + 1 + ts, :]
        o_ref[:, cols] = acc


def _hyena_short(conv_in, sc_w, sc_b, layer, dims, n_rows, ts):
    cw, hw = dims["CW"], dims["HW"]
    depth = sc_w.shape[0]
    n_lat = dims["T"] // ts
    col0 = 2 * cw // hw
    specs = []
    for m in range(3):
        specs += _halo_specs(ts, HY_HALO, hw, functools.partial(lambda m_: col0 + m_, m), conv_in.shape[0])
    kern = functools.partial(_hy_short_kernel, ts=ts, hw=hw, n_lat=n_lat, tps_lat=dims["S"] // ts,
                             tps_ctx=dims["Lc"] // ts)
    return pl.pallas_call(
        kern,
        out_shape=jax.ShapeDtypeStruct((n_rows, 3 * hw), F32),
        grid=(n_rows // ts,),
        in_specs=[*specs,
                  pl.BlockSpec((None, HY_SHORT_K, 3 * hw), lambda i: (layer, 0, 0)),
                  pl.BlockSpec((None, 1, 3 * hw), lambda i: (layer, 0, 0))],
        out_specs=pl.BlockSpec((ts, 3 * hw), lambda i: (i, 0)),
        scratch_shapes=[pltpu.VMEM((ts + 2 * HY_HALO, hw), F32)],
        compiler_params=_params("parallel"),
        name="hyena_short_conv",
    )(*([conv_in] * 9), sc_w, sc_b.reshape(depth, 1, 3 * hw))


def _filter_kernel(emb_ref, w1_ref, b1_ref, f_ref, w2_ref, b2_ref, w3_ref, dec_ref, hs_ref, hd_ref, asum_ref,
                   altb_ref, *, hw, tl):
    i = pl.program_id(0)
    hp = lax.Precision.HIGHEST
    emb = emb_ref[...]
    t = emb[:, 0:1]
    h1 = jnp.sin(f_ref[0:1, :] * (jnp.dot(emb, w1_ref[...], precision=hp, preferred_element_type=F32) + b1_ref[...]))
    h2 = jnp.sin(f_ref[1:2, :] * (jnp.dot(h1, w2_ref[...], precision=hp, preferred_element_type=F32) + b2_ref[...]))
    h = jnp.dot(h2, w3_ref[...], precision=hp, preferred_element_type=F32)
    h = h * jnp.exp(-t * jnp.abs(dec_ref[...]))
    col = lax.broadcasted_iota(jnp.int32, h.shape, 1)
    row = lax.broadcasted_iota(jnp.int32, h.shape, 0) + i * tl
    backward = (col // hw) % 2 == 1
    h = jnp.where(jnp.logical_and(backward, row == 0), 0.0, h)

    @pl.when(i == 0)
    def _():
        asum_ref[...] = jnp.zeros_like(asum_ref)
        altb_ref[...] = jnp.zeros_like(altb_ref)

    def fold(v):
        return v.reshape(tl // SUBLANES, SUBLANES, v.shape[1]).sum(axis=0)

    asum_ref[...] += fold(jnp.abs(h))
    alt = (1 - 2 * ((lax.broadcasted_iota(jnp.int32, (tl, hw), 0) + i * tl) & 1)).astype(F32)
    for o in range(h.shape[1] // (2 * hw)):
        hf = h[:, 2 * o * hw:(2 * o + 1) * hw]
        hb = h[:, (2 * o + 1) * hw:(2 * o + 2) * hw]
        sl = slice(o * hw, (o + 1) * hw)
        hs_ref[:, sl] = (hf + hb).astype(hs_ref.dtype)
        hd_ref[:, sl] = (hf - hb).astype(hd_ref.dtype)
        altb_ref[:, sl] += fold(hb * alt)


def _hyena_filter(emb, w1, b1, freq, w2, b2, w3, decay, layer, hw):
    l, ke = emb.shape
    depth, _, ffn = w1.shape
    nf = w3.shape[2]
    tl = _largest_tile((512, 256, 128), l)
    w1p = jnp.pad(w1, ((0, 0), (0, ke - w1.shape[1]), (0, 0)))
    vec = lambda n: pl.BlockSpec((None, 1, n), lambda i: (layer, 0, 0))
    half = jax.ShapeDtypeStruct((l, nf // 2), BF16)
    return pl.pallas_call(
        functools.partial(_filter_kernel, hw=hw, tl=tl),
        out_shape=(half, half, jax.ShapeDtypeStruct((SUBLANES, nf), F32),
                   jax.ShapeDtypeStruct((SUBLANES, nf // 2), F32)),
        grid=(l // tl,),
        in_specs=[
            pl.BlockSpec((tl, ke), lambda i: (i, 0)),
            pl.BlockSpec((None, ke, ffn), lambda i: (layer, 0, 0)),
            vec(ffn),
            pl.BlockSpec((None, 2, ffn), lambda i: (layer, 0, 0)),
            pl.BlockSpec((None, ffn, ffn), lambda i: (layer, 0, 0)),
            vec(ffn),
            pl.BlockSpec((None, ffn, nf), lambda i: (layer, 0, 0)),
            vec(nf),
        ],
        out_specs=(pl.BlockSpec((tl, nf // 2), lambda i: (i, 0)), pl.BlockSpec((tl, nf // 2), lambda i: (i, 0)),
                   pl.BlockSpec((SUBLANES, nf), lambda i: (0, 0)), pl.BlockSpec((SUBLANES, nf // 2), lambda i: (0, 0))),
        compiler_params=_params("arbitrary"),
        name="hyena_filter_mlp",
    )(emb, w1p, b1.reshape(depth, 1, ffn), freq, w2, b2.reshape(depth, 1, ffn), w3, decay.reshape(depth, 1, nf))


def _bin_scale(i, tm, n_bins):
    row = lax.broadcasted_iota(jnp.int32, (tm, 1), 0) + i * tm
    is0 = row == 0
    return is0, jnp.where(is0, 1.0 / (2 * n_bins), 2.0 / (2 * n_bins)).astype(F32)


def _filter_dft_kernel(c_ref, s_ref, hs_ref, hd_ref, af_ref, ab_ref, altb_ref, kre_ref, kim_ref, *, tm, n_bins):
    i = pl.program_id(1)
    re = _dot(c_ref[...], hs_ref[...])
    im = _dot(s_ref[...], hd_ref[...])
    norm = jnp.sum(af_ref[...], axis=0, keepdims=True) + jnp.sum(ab_ref[...], axis=0, keepdims=True)
    nyq_b = jnp.sum(altb_ref[...], axis=0, keepdims=True)
    is0, cs = _bin_scale(i, tm, n_bins)
    w = cs / norm
    kre_ref[...] = re * w
    kim_ref[...] = jnp.where(is0, im + 2.0 * nyq_b, im) * w


def _filter_spectrum(hs, hd, asum, altb, c_tab, s_tab, hw):
    l, n = hs.shape
    n_ord = n // hw
    tm = _largest_tile((512, 256, 128), l)
    out = jax.ShapeDtypeStruct((l, n_ord * hw), F32)
    tab = pl.BlockSpec((tm, l), lambda o, i: (i, 0))
    col = pl.BlockSpec((l, hw), lambda o, i: (0, o))
    return pl.pallas_call(
        functools.partial(_filter_dft_kernel, tm=tm, n_bins=l),
        out_shape=(out, out),
        grid=(n_ord, l // tm),
        in_specs=[tab, tab, col, col,
                  pl.BlockSpec((SUBLANES, hw), lambda o, i: (0, 2 * o)),
                  pl.BlockSpec((SUBLANES, hw), lambda o, i: (0, 2 * o + 1)),
                  pl.BlockSpec((SUBLANES, hw), lambda o, i: (0, o))],
        out_specs=(pl.BlockSpec((tm, hw), lambda o, i: (i, o)), pl.BlockSpec((tm, hw), lambda o, i: (i, o))),
        compiler_params=_params("parallel", "arbitrary"),
        name="hyena_filter_dft",
    )(c_tab, s_tab, hs, hd, asum, asum, altb)


def _fwd_dft_kernel(c_ref, s_ref, z_ref, kre_ref, kim_ref, yre_ref, yim_ref, zb, *, tm):
    i = pl.program_id(1)

    @pl.when(i == 0)
    def _():
        zb[...] = z_ref[...].astype(BF16)

    xre = _dot(c_ref[...], zb[...])
    xim = _dot(s_ref[...], zb[...])
    kre, kim = kre_ref[...], kim_ref[...]
    is0 = (lax.broadcasted_iota(jnp.int32, (tm, 1), 0) + i * tm) == 0
    yre = xre * kre - jnp.where(is0, 0.0, xim * kim)
    yim = jnp.where(is0, xim * kim, xre * kim + xim * kre)
    yre_ref[...] = yre.astype(BF16)
    yim_ref[...] = yim.astype(BF16)


def _fwd_dft(z_arr, z_row0, z_col, kre, kim, order, c_tab, s_tab, b, l, hw):
    tm = _largest_tile((512, 256, 128), l)
    out = jax.ShapeDtypeStruct((b * l, hw), BF16)
    tab = pl.BlockSpec((tm, l), lambda bi, i: (i, 0))
    kspec = pl.BlockSpec((tm, hw), lambda bi, i: (i, order))
    ospec = pl.BlockSpec((tm, hw), lambda bi, i: (bi * (l // tm) + i, 0))
    return pl.pallas_call(
        functools.partial(_fwd_dft_kernel, tm=tm),
        out_shape=(out, out),
        grid=(b, l // tm),
        in_specs=[tab, tab, pl.BlockSpec((l, hw), lambda bi, i: (z_row0 // l + bi, z_col)), kspec, kspec],
        out_specs=(ospec, ospec),
        scratch_shapes=[pltpu.VMEM((l, hw), BF16)],
        compiler_params=_params("parallel", "arbitrary"),
        name="hyena_fwd_dft",
    )(c_tab, s_tab, z_arr, kre, kim)


def _inv_dft_kernel(c_ref, st_ref, yre_ref, yim_ref, gate_ref, z_ref, bias_ref, o_ref):
    y = _dot(c_ref[...], yre_ref[...]) + _dot(st_ref[...], yim_ref[...])
    o_ref[...] = (gate_ref[...] * (y + bias_ref[...] * z_ref[...])).astype(o_ref.dtype)


def _inv_dft(yre, yim, u, u_row0, gate_col, z_arr, z_row0, z_col, bias, layer, order, c_tab, st_tab, b, l, hw,
             out_dtype):
    tm = _largest_tile((512, 256, 128), l)
    per = l // tm
    tab = pl.BlockSpec((tm, l), lambda bi, i: (i, 0))
    yspec = pl.BlockSpec((l, hw), lambda bi, i: (bi, 0))
    return pl.pallas_call(
        _inv_dft_kernel,
        out_shape=jax.ShapeDtypeStruct((b * l, hw), out_dtype),
        grid=(b, per),
        in_specs=[tab, tab, yspec, yspec,
                  pl.BlockSpec((tm, hw), lambda bi, i: (u_row0 // tm + bi * per + i, gate_col)),
                  pl.BlockSpec((tm, hw), lambda bi, i: (z_row0 // tm + bi * per + i, z_col)),
                  pl.BlockSpec((None, None, 1, hw), lambda bi, i: (layer, order, 0, 0))],
        out_specs=pl.BlockSpec((tm, hw), lambda bi, i: (bi * per + i, 0)),
        compiler_params=_params("parallel", "arbitrary"),
        name="hyena_inv_dft",
    )(c_tab, st_tab, yre, yim, u, z_arr, bias.reshape(bias.shape[0], bias.shape[1], 1, hw))


def _dft_table_kernel(c_ref, s_ref, st_ref, *, tm, l):
    n = 2 * l
    w = 2.0 * math.pi / n
    row = lax.broadcasted_iota(jnp.int32, (tm, LANES), 0) + pl.program_id(0) * tm
    lane = lax.broadcasted_iota(jnp.int32, (tm, LANES), 1)
    fine = ((row * lane) & (n - 1)).astype(F32) * w
    coarse = ((row * (lane * LANES)) & (n - 1)).astype(F32) * w
    cb, sb = jnp.cos(fine), jnp.sin(fine)
    ca, sa = jnp.cos(coarse), jnp.sin(coarse)
    alt_row = (1 - 2 * (row & 1)).astype(F32)
    for j in range(l // LANES):
        caj = jnp.broadcast_to(ca[:, j:j + 1], (tm, LANES))
        saj = jnp.broadcast_to(sa[:, j:j + 1], (tm, LANES))
        col = lane + j * LANES
        cos_blk = caj * cb - saj * sb
        nsin_blk = -(saj * cb + caj * sb)
        alt_col = (1 - 2 * (col & 1)).astype(F32)
        sl = slice(j * LANES, (j + 1) * LANES)
        c_ref[:, sl] = cos_blk.astype(BF16)
        s_ref[:, sl] = jnp.where(row == 0, alt_col, nsin_blk).astype(BF16)
        st_ref[:, sl] = jnp.where(col == 0, alt_row, nsin_blk).astype(BF16)


def _dft_tables(l):
    assert l % LANES == 0 and (l & (l - 1)) == 0 and l // LANES <= LANES
    tm = _largest_tile((256, 128), l)
    out = jax.ShapeDtypeStruct((l, l), BF16)
    spec = pl.BlockSpec((tm, l), lambda i: (i, 0))
    return pl.pallas_call(
        functools.partial(_dft_table_kernel, tm=tm, l=l),
        out_shape=(out, out, out),
        grid=(l // tm,),
        out_specs=(spec, spec, spec),
        compiler_params=_params("parallel"),
        name="dft_tables",
    )()


def _hyena_embedding(l):
    n = jnp.arange(l, dtype=F32)
    t = n / max(l - 1, 1)
    w = 2.0 * math.pi * n / l
    f = jnp.linspace(1e-4, HY_BANDS - 1, HY_BANDS, dtype=F32)
    fw = w[:, None] * f[None, :]
    emb = jnp.concatenate([t[:, None], jnp.cos(fw), -jnp.sin(fw)], axis=-1)
    return jnp.pad(emb, ((0, 0), (0, LANES - HY_EMB)))


def _hyena_long(u, u_row0, b, l, layer, p, tables, emb, dims):
    hw = dims["HW"]
    c_tab, s_tab, st_tab = tables
    hs, hd, asum, altb = _hyena_filter(emb, p["hy_w1"], p["hy_b1"], p["hy_freq"], p["hy_w2"], p["hy_b2"],
                                       p["hy_w3"], p["hy_decay"], layer, hw)
    kre, kim = _filter_spectrum(hs, hd, asum, altb, c_tab, s_tab, hw)
    yre, yim = _fwd_dft(u, u_row0, 2, kre, kim, 0, c_tab, s_tab, b, l, hw)
    z1 = _inv_dft(yre, yim, u, u_row0, 0, u, u_row0, 2, p["hy_bias"], layer, 0, c_tab, st_tab, b, l, hw, F32)
    yre, yim = _fwd_dft(z1, 0, 0, kre, kim, 1, c_tab, s_tab, b, l, hw)
    return _inv_dft(yre, yim, u, u_row0, 1, z1, 0, 0, p["hy_bias"], layer, 1, c_tab, st_tab, b, l, hw, BF16)


def _merge_kernel(attl_ref, attc_ref, conf_ref, hyl_ref, hyc_ref, wa_ref, wc_ref, wh_ref, g0_ref, g1_ref, g2_ref,
                  o_ref, *, n_lat):
    is_lat = pl.program_id(0) < n_lat
    a = _dot(jnp.where(is_lat, attl_ref[...], attc_ref[...]), wa_ref[...])
    c = _dot(conf_ref[...], wc_ref[...])
    h = _dot(jnp.where(is_lat, hyl_ref[...], hyc_ref[...]), wh_ref[...])
    m = g0_ref[...].astype(F32) * a + g1_ref[...].astype(F32) * c + g2_ref[...].astype(F32) * h
    o_ref[...] = m.astype(o_ref.dtype)


def _merge(att, att_ctx, conf, hy, hy_ctx, gates, w_attn_o, w_conf_o, w_hy_o, layer, n_rows, n_lat_rows, d, tm, tn):
    aw, cw, hw = att.shape[1], conf.shape[1], hy.shape[1]
    gb = d // tn
    n_lat = n_lat_rows // tm
    lat_map = lambda i, j: (jnp.minimum(i, n_lat - 1), 0)
    ctx_map = lambda i, j: (jnp.maximum(i - n_lat, 0), 0)
    return pl.pallas_call(
        functools.partial(_merge_kernel, n_lat=n_lat),
        out_shape=jax.ShapeDtypeStruct((n_rows, d), BF16),
        grid=(n_rows // tm, d // tn),
        in_specs=[
            pl.BlockSpec((tm, aw), lat_map),
            pl.BlockSpec((tm, aw), ctx_map),
            pl.BlockSpec((tm, cw), lambda i, j: (i, 0)),
            pl.BlockSpec((tm, hw), lat_map),
            pl.BlockSpec((tm, hw), ctx_map),
            pl.BlockSpec((aw, tn), lambda i, j: (0, j)),
            pl.BlockSpec((cw, tn), lambda i, j: (0, j)),
            pl.BlockSpec((hw, tn), lambda i, j: (0, j)),
            pl.BlockSpec((tm, tn), lambda i, j: (i, j)),
            pl.BlockSpec((tm, tn), lambda i, j: (i, gb + j)),
            pl.BlockSpec((tm, tn), lambda i, j: (i, 2 * gb + j)),
        ],
        out_specs=pl.BlockSpec((tm, tn), lambda i, j: (i, j)),
        compiler_params=_params("parallel", "arbitrary"),
        name="branch_proj_merge",
    )(att, att_ctx, conf, hy, hy_ctx, w_attn_o, w_conf_o, w_hy_o, gates, gates, gates)


def _out_proj_kernel(a_ref, w_ref, xa_ref, xb_ref, g_ref, o_ref, *, n_first):
    x = jnp.where(pl.program_id(0) < n_first, xa_ref[...], xb_ref[...])
    o_ref[...] = x + g_ref[...] * _dot(a_ref[...], w_ref[...])


def _out_proj(merged, w_out_bf16, x_pair, mod3, layer, n_rows, dims, tm, tn):
    xa, xb = x_pair
    d = xa.shape[1]
    n_lat = dims["T"] // tm
    n_first = xa.shape[0] // tm
    mm = functools.partial(_mod_row_map, layer, tm, n_lat, dims["S"], dims["B"])

    def gate_map(i, j):
        row, z, _ = mm(0)(i)
        return (row, z, 2 * (d // tn) + j)

    return pl.pallas_call(
        functools.partial(_out_proj_kernel, n_first=n_first),
        out_shape=jax.ShapeDtypeStruct((n_rows, d), F32),
        grid=(n_rows // tm, d // tn),
        in_specs=[
            pl.BlockSpec((tm, d), lambda i, j: (i, 0)),
            pl.BlockSpec((d, tn), lambda i, j: (0, j)),
            pl.BlockSpec((tm, tn), lambda i, j: (jnp.minimum(i, n_first - 1), j)),
            pl.BlockSpec((tm, tn), lambda i, j: (jnp.maximum(i - n_first, 0), j)),
            pl.BlockSpec((None, 1, tn), gate_map),
        ],
        out_specs=pl.BlockSpec((tm, tn), lambda i, j: (i, j)),
        compiler_params=_params("parallel", "arbitrary"),
        name="out_proj_residual",
    )(merged, w_out_bf16, xa, xb, mod3)


def _cast_kernel(w_ref, o_ref):
    o_ref[...] = w_ref[...].astype(o_ref.dtype)


def _layer_weight_bf16(w, layer):
    _, k, n = w.shape
    tk = _largest_tile((512, 256, 128), k)
    return pl.pallas_call(
        _cast_kernel,
        out_shape=jax.ShapeDtypeStruct((k, n), BF16),
        grid=(k // tk,),
        in_specs=[pl.BlockSpec((None, tk, n), lambda i: (layer, i, 0))],
        out_specs=pl.BlockSpec((tk, n), lambda i: (i, 0)),
        compiler_params=_params("parallel"),
        name="weight_to_bf16",
    )(w)


def _store_token_tiled(ref, x):
    n, d = x.shape
    ns = d // LANES
    for s in range(ns):
        ref[pl.ds(s, n, stride=ns), :] = x[:, s * LANES:(s + 1) * LANES]


def _router_kernel(x_ref, g_ref, sh_ref, sc_ref, wr_ref, h_ref, r_ref):
    h = _ada_norm_value(x_ref[...], g_ref[...], sh_ref[...], sc_ref[...])
    _store_token_tiled(h_ref, h)
    logits = _dot3(h, wr_ref[...])
    lane = lax.broadcasted_iota(jnp.int32, logits.shape, 1).astype(F32)
    ninf = jnp.float32(-jnp.inf)
    big = jnp.float32(LANES)

    def first_argmax(v):
        mx = jnp.max(v, axis=-1, keepdims=True)
        idx = jnp.min(jnp.where(v == mx, lane, big), axis=-1, keepdims=True)
        return mx, idx

    lg = jnp.where(lane < N_GROUPS, logits, ninf)
    gmax, gidx = first_argmax(lg)
    p_grp = 1.0 / jnp.sum(jnp.exp(lg - gmax), axis=-1, keepdims=True)
    lo = N_GROUPS + gidx * E_PER_GROUP
    le = jnp.where(jnp.logical_and(lane >= lo, lane < lo + E_PER_GROUP), logits, ninf)
    e1, i1 = first_argmax(le)
    e2, i2 = first_argmax(jnp.where(lane == i1, ninf, le))
    t = jnp.exp(e2 - e1)
    w1 = p_grp / (1.0 + t)
    w2 = p_grp * t / (1.0 + t)
    out = jnp.where(lane == 0, i1 - N_GROUPS,
                    jnp.where(lane == 1, i2 - N_GROUPS,
                              jnp.where(lane == 2, w1, jnp.where(lane == 3, w2, 0.0))))
    r_ref[...] = out


def _router(x_all, norm_g, mod3, w_router, layer, n_rows, dims, tm):
    d = x_all.shape[1]
    ns = d // LANES
    n_lat = dims["T"] // tm
    mm = functools.partial(_mod_row_map, layer, tm, n_lat, dims["S"], dims["B"])
    return pl.pallas_call(
        _router_kernel,
        out_shape=(jax.ShapeDtypeStruct((n_rows * ns, LANES), F32), jax.ShapeDtypeStruct((n_rows, LANES), F32)),
        grid=(n_rows // tm,),
        in_specs=[
            pl.BlockSpec((tm, d), lambda i: (i, 0)),
            pl.BlockSpec((None, 1, d), lambda i: (layer, 0, 0)),
            pl.BlockSpec((None, 1, d), mm(3)),
            pl.BlockSpec((None, 1, d), mm(4)),
            pl.BlockSpec((d, LANES), lambda i: (0, 0)),
        ],
        out_specs=(pl.BlockSpec((tm * ns, LANES), lambda i: (i, 0)), pl.BlockSpec((tm, LANES), lambda i: (i, 0))),
        compiler_params=_params("parallel"),
        name="adanorm2_router",
    )(x_all, norm_g.reshape(norm_g.shape[0], 1, d), mod3, mod3, w_router)


def _dispatch(route, n_tok, bm):
    eid = route[:, 0:2].astype(jnp.int32).reshape(-1)
    a = 2 * n_tok
    e_s, order = lax.sort_key_val(eid, jnp.arange(a, dtype=jnp.int32))
    experts = jnp.arange(N_EXPERTS, dtype=jnp.int32)
    cstart = jnp.searchsorted(e_s, experts, side='left').astype(jnp.int32)
    counts = jnp.searchsorted(e_s, experts, side='right').astype(jnp.int32) - cstart
    pcounts = (counts + bm - 1) // bm * bm
    pend = jnp.cumsum(pcounts)
    pstart = pend - pcounts
    n_blocks = -(-a // bm) + N_EXPERTS
    blk = jnp.arange(n_blocks, dtype=jnp.int32)
    blk_e = jnp.minimum(jnp.searchsorted(pend, blk * bm, side='right'), N_EXPERTS - 1).astype(jnp.int32)
    k = blk - pstart[blk_e] // bm
    blk_src0 = jnp.clip(cstart[blk_e] + k * bm, 0, a - 1).astype(jnp.int32)
    blk_nvalid = jnp.clip(counts[blk_e] - k * bm, 0, bm).astype(jnp.int32)
    n_used = (pend[-1] // bm).astype(jnp.int32).reshape(1)
    return order, blk_e, blk_src0, blk_nvalid, n_used, n_blocks


def _token_copy(src, src_tok, dst, dst_tok, ns, sem):
    return pltpu.make_async_copy(src.at[pl.ds(pl.multiple_of(src_tok * ns, ns), ns), :],
                                 dst.at[pl.ds(pl.multiple_of(dst_tok * ns, ns), ns), :], sem)


def _expert_kernel(tok_ref, order_ref, be_ref, src0_ref, nvalid_ref, nused_ref, h_hbm, wg_ref, wu_ref, wd_ref,
                   y_hbm, wg_s, wu_s, wd_s, xbuf, xb, obuf, gsem, ssem, *, bm, ns, n_assign, n_blocks):
    i = pl.program_id(0)
    slot = i % 2
    n_used = nused_ref[0]
    changed = jnp.logical_or(i == 0, be_ref[i] != be_ref[jnp.maximum(i - 1, 0)])

    def gather(block, sl):
        base = src0_ref[block]
        for r in range(bm):
            _token_copy(h_hbm, tok_ref[base + r], xbuf.at[sl], r, ns, gsem.at[sl]).start(priority=r % 2)

    def wait_all(buf, hbm, sem, sl):
        pltpu.make_async_copy(buf.at[sl], hbm.at[pl.ds(0, bm * ns), :], sem.at[sl]).wait()

    @pl.when(i == 0)
    def _():
        gather(0, 0)

    @pl.when(i >= 2)
    def _():
        wait_all(obuf, y_hbm, ssem, slot)

    @pl.when(i <= n_used)
    def _():
        wait_all(xbuf, h_hbm, gsem, slot)

    @pl.when(i < n_used)
    def _():
        for s in range(ns):
            xb[:, s * LANES:(s + 1) * LANES] = xbuf[slot, pl.ds(s, bm, stride=ns), :].astype(BF16)

        @pl.when(changed)
        def _():
            wg_s[...] = wg_ref[...].astype(BF16)
            wu_s[...] = wu_ref[...].astype(BF16)
            wd_s[...] = wd_ref[...].astype(BF16)

        gather(i + 1, 1 - slot)
        x = xb[...]
        g = _dot(x, wg_s[...])
        u = _dot(x, wu_s[...])
        mid = (g * jax.nn.sigmoid(g) * u).astype(BF16)
        _store_token_tiled(obuf.at[slot], _dot(mid, wd_s[...]))

    @pl.when(i >= n_used)
    def _():
        obuf[slot] = jnp.zeros(obuf.shape[1:], obuf.dtype)

    base = src0_ref[i]
    n_real = nvalid_ref[i]

    for r in range(bm):
        dst = jnp.where(r < n_real, order_ref[base + r], n_assign + slot * bm + r)
        _token_copy(obuf.at[slot], r, y_hbm, dst, ns, ssem.at[slot]).start(priority=r % 2)

    @pl.when(i == n_blocks - 1)
    def _():
        if n_blocks > 1:
            wait_all(obuf, y_hbm, ssem, 1 - slot)
        wait_all(obuf, y_hbm, ssem, slot)

        @pl.when(i < n_used)
        def _():
            wait_all(xbuf, h_hbm, gsem, 1 - slot)


def _experts(h2t, order, blk_e, blk_src0, blk_nvalid, n_used, w_gate, w_up, w_down, layer, n_blocks, bm, d):
    f = w_gate.shape[3]
    ns = d // LANES
    n_assign = order.shape[0]
    order_p = jnp.concatenate([order, jnp.zeros((bm,), jnp.int32)])
    tok_p = order_p // 2
    blk_src0 = jnp.concatenate([blk_src0, jnp.zeros((1,), jnp.int32)])
    wspec = lambda shape: pl.BlockSpec((None, None) + shape, lambda i, t, o, be, *_: (layer, be[i], 0, 0))
    return pl.pallas_call(
        functools.partial(_expert_kernel, bm=bm, ns=ns, n_assign=n_assign, n_blocks=n_blocks),
        out_shape=jax.ShapeDtypeStruct(((n_assign + 2 * bm) * ns, LANES), F32),
        grid_spec=pltpu.PrefetchScalarGridSpec(
            num_scalar_prefetch=6,
            grid=(n_blocks,),
            in_specs=[pl.BlockSpec(memory_space=pl.ANY), wspec((d, f)), wspec((d, f)), wspec((f, d))],
            out_specs=pl.BlockSpec(memory_space=pl.ANY),
            scratch_shapes=[pltpu.VMEM((d, f), BF16), pltpu.VMEM((d, f), BF16), pltpu.VMEM((f, d), BF16),
                            pltpu.VMEM((2, bm * ns, LANES), F32), pltpu.VMEM((bm, d), BF16),
                            pltpu.VMEM((2, bm * ns, LANES), F32),
                            pltpu.SemaphoreType.DMA((2,)), pltpu.SemaphoreType.DMA((2,))],
        ),
        compiler_params=_params("arbitrary"),
        name="moe_experts",
    )(tok_p, order_p, blk_e, blk_src0, blk_nvalid, n_used, h2t, w_gate, w_up, w_down)


def _combine_kernel(y_ref, x_ref, g_ref, route_ref, *rest, tc, ns, final):
    if final:
        ng_ref, o_ref = rest
    else:
        (o_ref,) = rest
    w0 = route_ref[:, 2:3]
    w1 = route_ref[:, 3:4]
    ssq = jnp.zeros((tc, 1), F32)
    for s in range(ns):
        sl = slice(s * LANES, (s + 1) * LANES)
        y = w0 * y_ref[pl.ds(s, tc, stride=2 * ns), :] + w1 * y_ref[pl.ds(ns + s, tc, stride=2 * ns), :]
        xn = x_ref[:, sl] + g_ref[:, sl] * y
        if final:
            ssq = ssq + jnp.sum(xn * xn, axis=-1, keepdims=True)
        o_ref[:, sl] = xn
    if final:
        o_ref[...] = o_ref[...] * lax.rsqrt(ssq / (ns * LANES) + EPS) * ng_ref[...]


def _combine(ys, route, x_all, mod3, layer, n_rows, dims, tc, norm_f_g=None):
    d = x_all.shape[1]
    ns = d // LANES
    n_lat = dims["T"] // tc
    mm = functools.partial(_mod_row_map, layer, tc, n_lat, dims["S"], dims["B"])
    final = norm_f_g is not None
    in_specs = [
        pl.BlockSpec((tc * 2 * ns, LANES), lambda i: (i, 0)),
        pl.BlockSpec((tc, d), lambda i: (i, 0)),
        pl.BlockSpec((None, 1, d), mm(5)),
        pl.BlockSpec((tc, LANES), lambda i: (i, 0)),
    ]
    args = [ys, x_all, mod3, route]
    if final:
        in_specs.append(pl.BlockSpec((1, d), lambda i: (0, 0)))
        args.append(norm_f_g.reshape(1, d))
    return pl.pallas_call(
        functools.partial(_combine_kernel, tc=tc, ns=ns, final=final),
        out_shape=jax.ShapeDtypeStruct((n_rows, d), F32),
        grid=(n_rows // tc,),
        in_specs=in_specs,
        out_specs=pl.BlockSpec((tc, d), lambda i: (i, 0)),
        compiler_params=_params("parallel"),
        name="moe_combine_residual",
    )(*args)


def _rope_table(s):
    half = ATT_DQK // 2
    rows = jnp.repeat(jnp.arange(s // GRID_W, dtype=jnp.int32), GRID_W)
    cols = jnp.tile(jnp.arange(GRID_W, dtype=jnp.int32), s // GRID_W)
    inv = ROPE_BASE ** (-jnp.arange(0, half, 2, dtype=F32) / half)
    ar = rows.astype(F32)[:, None] * inv
    ac = cols.astype(F32)[:, None] * inv
    cr, sr, cc, sc = jnp.cos(ar), jnp.sin(ar), jnp.cos(ac), jnp.sin(ac)
    z = jnp.zeros_like(sr)
    c = jnp.concatenate([cr, cr, cc, cc] * 2, axis=-1)
    s1 = jnp.concatenate([-sr, z, -sc, z] * 2, axis=-1)
    s2 = jnp.concatenate([z, sr, z, sc] * 2, axis=-1)
    return jnp.concatenate([c, s1, s2], axis=-1)


def kernel(x, c, ctx, c_ctx, w_mod, b_mod, norm1_g, norm2_g, w_in, lam_q1, lam_k1, lam_q2, lam_k2, attn_subln_g, w_attn_o, conf_dw_w, conf_dw_b, conf_ln_g, conf_ln_b, w_conf_o, hy_sc_w, hy_sc_b, hy_w1, hy_b1, hy_w2, hy_b2, hy_freq, hy_w3, hy_decay, hy_bias, w_hy_o, w_out, w_router_group, w_router_expert, w_exp_gate, w_exp_up, w_exp_down, norm_f_g):
    b, s, d = x.shape
    lc = ctx.shape[1]
    depth = w_mod.shape[0]
    t, tc_rows = b * s, b * lc
    cw, hw = conf_dw_w.shape[2], w_hy_o.shape[1]
    dims = dict(B=b, S=s, Lc=lc, T=t, QK=ATT_HEADS * 2 * ATT_DQK, AW=ATT_HEADS * ATT_DV, CW=cw, HW=hw)
    assert b + 1 <= MOD_ROWS and 2 * ATT_DQK == LANES and ATT_DV == LANES and hw == cw
    tm = _largest_tile((1024, 512, 256, 128), s, tc_rows)
    ts = _largest_tile((256, 128), s, lc)
    tn = _largest_tile((512, 256, 128), dims["QK"], dims["AW"], 2 * cw, 3 * hw, d)
    tcomb = _largest_tile((256, 128), s, tc_rows)

    x_pair = (x.reshape(t, d), ctx.reshape(tc_rows, d))
    cond = jnp.zeros((MOD_ROWS, d), F32).at[:b].set(c).at[b].set(c_ctx)
    mod3 = _modulation(cond, w_mod, b_mod).reshape(depth * MOD_ROWS, 1, 6 * d)
    rope_tab = _rope_table(s)
    tables_lat, emb_lat = _dft_tables(s), _hyena_embedding(s)
    tables_ctx, emb_ctx = _dft_tables(lc), _hyena_embedding(lc)
    hy_params = dict(hy_w1=hy_w1, hy_b1=hy_b1, hy_w2=hy_w2, hy_b2=hy_b2, hy_freq=hy_freq, hy_w3=hy_w3,
                     hy_decay=hy_decay, hy_bias=hy_bias)

    out = None
    for l in range(depth):
        need_ctx = l < depth - 1
        n_rows = t + tc_rows if need_ctx else t
        lam_init = 0.8 - 0.6 * math.exp(-0.3 * l)
        lam = (jnp.exp(jnp.sum(lam_q1[l] * lam_k1[l])) - jnp.exp(jnp.sum(lam_q2[l] * lam_k2[l])) + lam_init)
        lam = lam.reshape(1).astype(F32)

        hl = _adanorm(x_pair, t + tc_rows, norm1_g, mod3, l, dims, tm)
        qkv, conv_in, gates = _in_proj(hl, w_in, l, rope_tab, dims, tm, tn)
        att, att_ctx = _attention(qkv, lam, attn_subln_g, l, lam_init, dims, need_ctx)
        conf = _conformer(conv_in, conf_dw_w, conf_dw_b, conf_ln_g, conf_ln_b, l, dims, n_rows, ts)
        u = _hyena_short(conv_in, hy_sc_w, hy_sc_b, l, dims, n_rows, ts)
        hy = _hyena_long(u, 0, b, s, l, hy_params, tables_lat, emb_lat, dims)
        hy_ctx = _hyena_long(u, t, b, lc, l, hy_params, tables_ctx, emb_ctx, dims) if need_ctx else hy
        merged = _merge(att, att_ctx, conf, hy, hy_ctx, gates, _layer_weight_bf16(w_attn_o, l),
                        _layer_weight_bf16(w_conf_o, l), _layer_weight_bf16(w_hy_o, l), l, n_rows, t, d, tm, tn)
        x_all = _out_proj(merged, _layer_weight_bf16(w_out, l), x_pair, mod3, l, n_rows, dims, tm, tn)

        w_router = jnp.concatenate(
            [w_router_group[l], jnp.transpose(w_router_expert[l], (1, 0, 2)).reshape(d, N_EXPERTS)], axis=1)
        w_router = jnp.pad(w_router, ((0, 0), (0, LANES - w_router.shape[1])))
        h2t, route = _router(x_all, norm2_g, mod3, w_router, l, n_rows, dims, tm)
        order, blk_e, blk_src0, blk_nvalid, n_used, n_blocks = _dispatch(route, n_rows, MOE_BLOCK)
        ys = _experts(h2t, order, blk_e, blk_src0, blk_nvalid, n_used, w_exp_gate, w_exp_up, w_exp_down, l,
                      n_blocks, MOE_BLOCK, d)
        x_new = _combine(ys, route, x_all, mod3, l, n_rows, dims, tcomb, None if need_ctx else norm_f_g)
        if need_ctx:
            x_pair = (x_new, x_new)
        else:
            out = x_new
    return out.reshape(b, s, d)
```
